```python
import math
import jax, jax.numpy as jnp
from jax import lax
import numpy as np

D_MODEL = 2048
BATCH = 4
SEQ = 8192
DEPTH = 1
DEC_BATCH = 8
DEC_SEQ = 2048
PAST_LEN = 128

D_ATT = D_MODEL
ATT_HEAD_DIM = 128
ATT_HEADS = D_ATT // (2 * ATT_HEAD_DIM)
Q_BLOCK = 128
D_SSM = D_MODEL
SSM_HEAD_DIM = 64
SSM_HEADS = D_SSM // SSM_HEAD_DIM
SSM_GROUPS = 4
SSM_STATE = 128
D_CONV = 5
CHUNK = 128
D_XBC = D_SSM + 2 * SSM_GROUPS * SSM_STATE
MEM_TOKENS = 256
MEM_HEADS = 4
MEM_HEAD_DIM = 128
D_MEM = MEM_HEADS * MEM_HEAD_DIM
D_MIX = D_ATT + D_SSM + D_MEM
IN_SIZES = (D_ATT, D_ATT, D_ATT, D_ATT, D_SSM, D_XBC, 2 * SSM_HEADS, D_MEM, D_MEM)
D_IN = sum(IN_SIZES)
EPS = 1e-5

kernel_name = "hybrid_diffattn_ssd_memory_encoder"


def _split(t, sizes):
    idx = np.cumsum(np.array(sizes))[:-1].tolist()
    return jnp.split(t, idx, axis=-1)


def layer_norm(x, g, b):
    xf = x.astype(jnp.float32)
    mu = jnp.mean(xf, axis=-1, keepdims=True)
    var = jnp.mean(jnp.square(xf - mu), axis=-1, keepdims=True)
    y = (xf - mu) * lax.rsqrt(var + EPS) * g.astype(jnp.float32) + b.astype(jnp.float32)
    return y.astype(x.dtype)


def rms_norm(x, g):
    xf = x.astype(jnp.float32)
    y = xf * lax.rsqrt(jnp.mean(jnp.square(xf), axis=-1, keepdims=True) + EPS)
    return (y * g.astype(jnp.float32)).astype(x.dtype)


def centered_depthwise_conv(u, w, bias):
    T = u.shape[1]
    pad = (D_CONV - 1) // 2
    up = jnp.pad(u, ((0, 0), (pad, pad), (0, 0)))
    out = up[:, 0:T] * w[0]
    for k in range(1, D_CONV):
        out = out + up[:, k:k + T] * w[k]
    return out + bias


def ssd_scan(x, dt, A, Bm, Cm):
    b, T, H, P = x.shape
    G, N = Bm.shape[2], Bm.shape[3]
    E = H // G
    nc = T // CHUNK
    xd = (x * dt[..., None]).reshape(b, nc, CHUNK, G, E, P)
    a = (dt * A).reshape(b, nc, CHUNK, G, E)
    Bc = Bm.reshape(b, nc, CHUNK, G, N)
    Cc = Cm.reshape(b, nc, CHUNK, G, N)
    a_cum = jnp.cumsum(a, axis=2)
    seg = a_cum[:, :, :, None] - a_cum[:, :, None, :]
    mask = jnp.tril(jnp.ones((CHUNK, CHUNK), dtype=bool))[None, None, :, :, None, None]
    Lmat = jnp.exp(jnp.where(mask, seg, -jnp.inf))
    CB = jnp.einsum("bclgn,bcsgn->bclsg", Cc, Bc)
    y_diag = jnp.einsum("bclsge,bcsgep->bclgep", CB[..., None] * Lmat, xd)
    decay_states = jnp.exp(a_cum[:, :, -1:] - a_cum)
    states = jnp.einsum("bclgn,bclge,bclgep->bcgepn", Bc, decay_states, xd)
    chunk_decay = jnp.exp(a_cum[:, :, -1])

    def step(h, inp):
        s, d = inp
        return h * d[..., None, None] + s, h

    h0 = jnp.zeros_like(states[:, 0])
    _, h_in = lax.scan(step, h0, (jnp.swapaxes(states, 0, 1), jnp.swapaxes(chunk_decay, 0, 1)))
    h_in = jnp.swapaxes(h_in, 0, 1)
    y_off = jnp.einsum("bclgn,bcgepn,bclge->bclgep", Cc, h_in, jnp.exp(a_cum))
    return (y_diag + y_off).reshape(b, T, H, P)


def ssm_branch(xbc, dt_raw, z, conv_w, conv_b, dt_bias, a_log, d_skip, norm_g):
    b, T, _ = xbc.shape
    xbc = jax.nn.silu(centered_depthwise_conv(xbc, conv_w, conv_b))
    xs, Bm, Cm = _split(xbc, (D_SSM, SSM_GROUPS * SSM_STATE, SSM_GROUPS * SSM_STATE))
    xs = xs.reshape(b, T, SSM_HEADS, SSM_HEAD_DIM)
    Bm = Bm.reshape(b, T, SSM_GROUPS, SSM_STATE)
    Cm = Cm.reshape(b, T, SSM_GROUPS, SSM_STATE)
    dt = jax.nn.softplus(dt_raw.reshape(b, T, 2, SSM_HEADS) + dt_bias)
    A = -jnp.exp(a_log.astype(jnp.float32))
    flip = lambda t: jnp.flip(t, axis=1)
    y_f = ssd_scan(xs, dt[:, :, 0], A[0], Bm, Cm)
    y_b = flip(ssd_scan(flip(xs), flip(dt[:, :, 1]), A[1], flip(Bm), flip(Cm)))
    y = (y_f + y_b + d_skip[:, None] * xs).astype(xs.dtype)
    y = y.reshape(b, T, D_SSM) * jax.nn.silu(z)
    y = rms_norm(y.reshape(b, T, SSM_GROUPS, D_SSM // SSM_GROUPS), jnp.ones((D_SSM // SSM_GROUPS,), y.dtype))
    return y.reshape(b, T, D_SSM) * norm_g


def diff_attention(q, k, v, lam, lam_init, subln_g):
    b, T, _ = q.shape
    H, d = ATT_HEADS, ATT_HEAD_DIM
    q = q.reshape(b, T, H, 2, d)
    k = k.reshape(b, T, H, 2, d)
    v = v.reshape(b, T, H, 2 * d)
    slopes = 2.0 ** (-8.0 * jnp.arange(1, H + 1, dtype=jnp.float32) / H)
    kpos = jnp.arange(T, dtype=jnp.int32)
    scale = 1.0 / math.sqrt(d)
    nb = T // Q_BLOCK
    qb = jnp.transpose(q.reshape(b, nb, Q_BLOCK, H, 2, d), (1, 0, 2, 3, 4, 5))
    starts = jnp.arange(nb, dtype=jnp.int32) * Q_BLOCK

    def block(args):
        qblk, q0 = args
        s = jnp.einsum("bqhmd,bkhmd->bhmqk", qblk, k).astype(jnp.float32) * scale
        qpos = q0 + jnp.arange(Q_BLOCK, dtype=jnp.int32)
        dist = jnp.abs(qpos[:, None] - kpos[None, :]).astype(jnp.float32)
        s = s - slopes[None, :, None, None, None] * dist[None, None, None]
        p = jax.nn.softmax(s, axis=-1)
        w = p[:, :, 0] - lam * p[:, :, 1]
        return jnp.einsum("bhqk,bkhe->bqhe", w.astype(v.dtype), v)

    out = lax.map(block, (qb, starts))
    out = jnp.transpose(out, (1, 0, 2, 3, 4)).reshape(b, T, H, 2 * d)
    out = rms_norm(out, subln_g) * (1.0 - lam_init)
    return out.reshape(b, T, D_ATT)


def memory_attention(q, k, v):
    b, T, _ = q.shape
    q = q.reshape(b, T, MEM_HEADS, MEM_HEAD_DIM)
    k = k.reshape(b, -1, MEM_HEADS, MEM_HEAD_DIM)
    v = v.reshape(b, -1, MEM_HEADS, MEM_HEAD_DIM)
    s = jnp.einsum("bqhd,bkhd->bhqk", q, k).astype(jnp.float32) / math.sqrt(MEM_HEAD_DIM)
    p = jax.nn.softmax(s, axis=-1)
    out = jnp.einsum("bhqk,bkhd->bqhd", p.astype(v.dtype), v)
    return out.reshape(b, T, D_MEM)


def encoder_layer(x, mem, layer_idx, w_in, conv_w, conv_b, dt_bias, a_log, d_skip, ssm_norm_g,
                  diff_lambda, subln_g, w_mem_kv, w_out, ln_g, ln_b):
    alpha = (2.0 * DEPTH) ** 0.25
    proj = jnp.einsum("btd,de->bte", x, w_in)
    q_att, k_att, v_att, g_att, z_ssm, xbc, dt_raw, q_mem, g_mem = _split(proj, IN_SIZES)
    lam_init = 0.8 - 0.6 * math.exp(-0.3 * layer_idx)
    dl = diff_lambda.astype(jnp.float32)
    lam = jnp.exp(jnp.sum(dl[0] * dl[1])) - jnp.exp(jnp.sum(dl[2] * dl[3])) + lam_init
    h_att = diff_attention(q_att, k_att, v_att, lam, lam_init, subln_g) * jax.nn.silu(g_att)
    h_ssm = ssm_branch(xbc, dt_raw, z_ssm, conv_w, conv_b, dt_bias, a_log, d_skip, ssm_norm_g)
    k_mem, v_mem = _split(jnp.einsum("bmd,de->bme", mem, w_mem_kv), (D_MEM, D_MEM))
    h_mem = memory_attention(q_mem, k_mem, v_mem) * jax.nn.silu(g_mem)
    h = jnp.concatenate([h_att, h_ssm, h_mem], axis=-1)
    out = jnp.einsum("bte,ed->btd", h, w_out)
    return layer_norm(alpha * x + out, ln_g, ln_b)


def setup_inputs(seed: int = 0) -> dict:
    key = jax.random.key(seed)
    ks = jax.random.split(key, 20)
    f32 = jnp.float32
    beta = (8.0 * DEPTH) ** -0.25
    col_scale = jnp.ones((D_IN,), f32).at[2 * D_ATT:3 * D_ATT].set(beta)
    w_in = jax.random.normal(ks[4], (DEPTH, D_MODEL, D_IN), f32) * D_MODEL ** -0.5 * col_scale
    dt0 = jnp.exp(jax.random.uniform(ks[7], (DEPTH, 2, SSM_HEADS), f32, math.log(1e-3), math.log(1e-1)))
    mem_scale = jnp.ones((2 * D_MEM,), f32).at[D_MEM:].set(beta)
    return {
        "x_prompt": jax.random.normal(ks[0], (BATCH, SEQ, D_MODEL), f32),
        "x_sample": jax.random.normal(ks[1], (DEC_BATCH, DEC_SEQ, D_MODEL), f32),
        "mem_prompt": jax.random.normal(ks[2], (BATCH, MEM_TOKENS, D_MODEL), f32),
        "mem_sample": jax.random.normal(ks[3], (DEC_BATCH, MEM_TOKENS, D_MODEL), f32),
        "ln_in_g": 1.0 + 0.02 * jax.random.normal(ks[5], (D_MODEL,), f32),
        "ln_in_b": 0.02 * jax.random.normal(ks[6], (D_MODEL,), f32),
        "w_in": w_in,
        "conv_w": jax.random.normal(ks[8], (DEPTH, D_CONV, D_XBC), f32) * D_CONV ** -0.5,
        "conv_b": 0.01 * jax.random.normal(ks[9], (DEPTH, D_XBC), f32),
        "dt_bias": dt0 + jnp.log(-jnp.expm1(-dt0)),
        "a_log": jnp.log(jax.random.uniform(ks[10], (DEPTH, 2, SSM_HEADS), f32, 1.0, 16.0)),
        "d_skip": 1.0 + 0.02 * jax.random.normal(ks[11], (DEPTH, SSM_HEADS), f32),
        "ssm_norm_g": 1.0 + 0.02 * jax.random.normal(ks[12], (DEPTH, D_SSM), f32),
        "diff_lambda": 0.1 * jax.random.normal(ks[13], (DEPTH, 4, ATT_HEAD_DIM), f32),
        "subln_g": 1.0 + 0.02 * jax.random.normal(ks[14], (DEPTH, 2 * ATT_HEAD_DIM), f32),
        "w_mem_kv": jax.random.normal(ks[15], (DEPTH, D_MODEL, 2 * D_MEM), f32) * D_MODEL ** -0.5 * mem_scale,
        "w_out": jax.random.normal(ks[16], (DEPTH, D_MIX, D_MODEL), f32) * D_MIX ** -0.5 * beta,
        "ln_g": 1.0 + 0.02 * jax.random.normal(ks[17], (DEPTH, D_MODEL), f32),
        "ln_b": 0.02 * jax.random.normal(ks[18], (DEPTH, D_MODEL), f32),
    }


def reference(x_prompt, x_sample, mem_prompt, mem_sample, ln_in_g, ln_in_b, w_in, conv_w, conv_b,
              dt_bias, a_log, d_skip, ssm_norm_g, diff_lambda, subln_g, w_mem_kv, w_out, ln_g, ln_b):
    def trunk(x, mem):
        x = layer_norm(x, ln_in_g, ln_in_b)
        for l in range(DEPTH):
            x = encoder_layer(x, mem, l, w_in[l], conv_w[l], conv_b[l], dt_bias[l], a_log[l], d_skip[l],
                              ssm_norm_g[l], diff_lambda[l], subln_g[l], w_mem_kv[l], w_out[l],
                              ln_g[l], ln_b[l])
        return x

    y_prompt = trunk(x_prompt, mem_prompt)
    y_sample = trunk(x_sample, mem_sample)
    return (y_prompt, y_sample)
```

```python
import functools
import math

import jax
import jax.numpy as jnp
from jax import lax
from jax.experimental import pallas as pl
from jax.experimental.pallas import tpu as pltpu

F32 = jnp.float32
BF16 = jnp.bfloat16

DEPTH = 1
EPS = 1e-5
ATT_HEADS = 8
ATT_HEAD_DIM = 128
ATT_V_DIM = 2 * ATT_HEAD_DIM
SSM_HEADS = 32
SSM_HEAD_DIM = 64
SSM_GROUPS = 4
SSM_STATE = 128
HEADS_PER_GROUP = SSM_HEADS // SSM_GROUPS
GROUP_WIDTH = HEADS_PER_GROUP * SSM_HEAD_DIM
D_CONV = 5
CHUNK = 128
MEM_HEADS = 4
MEM_HEAD_DIM = 128
LANES = 128
CONV_HALO_ROWS = 16
ALPHA = (2.0 * DEPTH) ** 0.25
VMEM_LIMIT_BYTES = 56 * 1024 * 1024


def _cparams(sem):
    return pltpu.CompilerParams(dimension_semantics=sem, vmem_limit_bytes=VMEM_LIMIT_BYTES)


def _layer_norm(x, g, b):
    mu = jnp.mean(x, axis=-1, keepdims=True)
    xc = x - mu
    var = jnp.mean(xc * xc, axis=-1, keepdims=True)
    return xc * lax.rsqrt(var + EPS) * g + b


def _silu(x):
    return x * jax.nn.sigmoid(x)


def _dot(a, b):
    return jnp.dot(a, b, preferred_element_type=F32)


def _dot_nt(a, b):
    return lax.dot_general(a, b, (((1,), (1,)), ((), ())), preferred_element_type=F32)


def _in_proj_kernel(x_ref, g_ref, b_ref, w_ref, wdt_ref, proj_ref, dt_ref, xn_ref):
    @pl.when(pl.program_id(1) == 0)
    def _():
        xn = _layer_norm(x_ref[...], g_ref[...], b_ref[...]).astype(BF16)
        xn_ref[...] = xn
        dt_ref[...] = _dot(xn, wdt_ref[...])

    proj_ref[...] = _dot(xn_ref[...], w_ref[...]).astype(BF16)


def _in_proj(x2, ln_g, ln_b, w_main, w_dt, *, tm, tn):
    m, d = x2.shape
    n = w_main.shape[1]
    return pl.pallas_call(
        _in_proj_kernel,
        out_shape=(jax.ShapeDtypeStruct((m, n), BF16), jax.ShapeDtypeStruct((m, LANES), F32)),
        grid=(m // tm, n // tn),
        in_specs=[
            pl.BlockSpec((tm, d), lambda i, j: (i, 0)),
            pl.BlockSpec((1, d), lambda i, j: (0, 0)),
            pl.BlockSpec((1, d), lambda i, j: (0, 0)),
            pl.BlockSpec((d, tn), lambda i, j: (0, j)),
            pl.BlockSpec((d, LANES), lambda i, j: (0, 0)),
        ],
        out_specs=(
            pl.BlockSpec((tm, tn), lambda i, j: (i, j)),
            pl.BlockSpec((tm, LANES), lambda i, j: (i, 0)),
        ),
        scratch_shapes=[pltpu.VMEM((tm, d), BF16)],
        compiler_params=_cparams(("parallel", "arbitrary")),
        name="in_proj",
    )(x2, ln_g, ln_b, w_main, w_dt)


def _attn_kernel(dl_ref, subg_ref, q_ref, k_ref, v_ref, g_ref, o_ref, acc_ref, m_ref, l_ref,
                 *, tq, tk, lam_init):
    h = pl.program_id(1)
    i = pl.program_id(2)
    seq = k_ref.shape[0]
    d = ATT_HEAD_DIM
    scale = 1.0 / math.sqrt(d)
    slope = jnp.exp2(-(h + 1).astype(F32))

    q = q_ref[...]
    q_maps = (q[:, :d], q[:, d:])
    rel = (lax.broadcasted_iota(jnp.int32, (tq, tk), 0)
           - lax.broadcasted_iota(jnp.int32, (tq, tk), 1) + i * tq)

    acc_ref[...] = jnp.zeros_like(acc_ref)
    l_ref[...] = jnp.zeros_like(l_ref)
    m_ref[...] = jnp.full_like(m_ref, -1e30)

    def body(j, carry):
        k_blk = k_ref[pl.ds(pl.multiple_of(j * tk, tk), tk), :]
        v_blk = v_ref[pl.ds(pl.multiple_of(j * tk, tk), tk), :]
        bias = jnp.abs(rel - j * tk).astype(F32) * slope
        for mp in range(2):
            s = _dot_nt(q_maps[mp], k_blk[:, mp * d:(mp + 1) * d]) * scale - bias
            m_old = m_ref[mp]
            m_new = jnp.maximum(m_old, jnp.max(s, axis=-1, keepdims=True))
            a = jnp.exp(m_old - m_new)
            p = jnp.exp(s - m_new)
            l_ref[mp] = a * l_ref[mp] + jnp.sum(p, axis=-1, keepdims=True)
            acc_ref[mp] = a * acc_ref[mp] + _dot(p.astype(BF16), v_blk)
            m_ref[mp] = m_new
        return carry

    lax.fori_loop(0, seq // tk, body, 0)

    dl = dl_ref[...]
    lam = (jnp.exp(jnp.sum(dl[0:1] * dl[1:2], axis=-1, keepdims=True))
           - jnp.exp(jnp.sum(dl[2:3] * dl[3:4], axis=-1, keepdims=True)) + lam_init)
    out = acc_ref[0] / l_ref[0] - lam * (acc_ref[1] / l_ref[1])
    ms = jnp.mean(out * out, axis=-1, keepdims=True)
    out = out * lax.rsqrt(ms + EPS) * subg_ref[...] * (1.0 - lam_init)
    o_ref[...] = (out * _silu(g_ref[...].astype(F32))).astype(BF16)


def _attention(proj3, diff_lambda, subln_g, *, lam_init, tq, tk):
    b, t, _ = proj3.shape
    hh = ATT_HEADS
    kern = functools.partial(_attn_kernel, tq=tq, tk=tk, lam_init=lam_init)
    return pl.pallas_call(
        kern,
        out_shape=jax.ShapeDtypeStruct((b, t, hh * ATT_V_DIM), BF16),
        grid=(b, hh, t // tq),
        in_specs=[
            pl.BlockSpec((4, ATT_HEAD_DIM), lambda bi, h, i: (0, 0)),
            pl.BlockSpec((1, ATT_V_DIM), lambda bi, h, i: (0, 0)),
            pl.BlockSpec((None, tq, ATT_V_DIM), lambda bi, h, i: (bi, i, h)),
            pl.BlockSpec((None, t, ATT_V_DIM), lambda bi, h, i: (bi, 0, hh + h)),
            pl.BlockSpec((None, t, ATT_V_DIM), lambda bi, h, i: (bi, 0, 2 * hh + h)),
            pl.BlockSpec((None, tq, ATT_V_DIM), lambda bi, h, i: (bi, i, 3 * hh + h)),
        ],
        out_specs=pl.BlockSpec((None, tq, ATT_V_DIM), lambda bi, h, i: (bi, i, h)),
        scratch_shapes=[
            pltpu.VMEM((2, tq, ATT_V_DIM), F32),
            pltpu.VMEM((2, tq, 1), F32),
            pltpu.VMEM((2, tq, 1), F32),
        ],
        compiler_params=_cparams(("parallel", "parallel", "arbitrary")),
        name="diff_attn",
    )(diff_lambda, subln_g, proj3, proj3, proj3, proj3)


def _conv_kernel(cur_ref, prev_ref, next_ref, w_ref, b_ref, o_ref, u_ref, *, tt):
    i = pl.program_id(1)
    halo = CONV_HALO_ROWS
    pad = (D_CONV - 1) // 2
    prev = prev_ref[...].astype(F32)
    nxt = next_ref[...].astype(F32)
    u_ref[0:halo, :] = jnp.where(i > 0, prev, 0.0)
    u_ref[halo:halo + tt, :] = cur_ref[...].astype(F32)
    u_ref[halo + tt:2 * halo + tt, :] = jnp.where(i < pl.num_programs(1) - 1, nxt, 0.0)
    acc = u_ref[pl.ds(halo - pad, tt), :] * w_ref[0:1, :]
    for k in range(1, D_CONV):
        acc = acc + u_ref[pl.ds(halo - pad + k, tt), :] * w_ref[k:k + 1, :]
    o_ref[...] = _silu(acc + b_ref[...]).astype(BF16)


def _conv(proj3, conv_w, conv_b, *, col0, tt, tc):
    b, t, _ = proj3.shape
    width = conv_w.shape[1]
    halo = CONV_HALO_ROWS
    cb0 = col0 // tc
    rpb = tt // halo
    last = t // halo - 1
    kern = functools.partial(_conv_kernel, tt=tt)
    return pl.pallas_call(
        kern,
        out_shape=jax.ShapeDtypeStruct((b, t, width), BF16),
        grid=(b, t // tt, width // tc),
        in_specs=[
            pl.BlockSpec((None, tt, tc), lambda bi, i, c: (bi, i, cb0 + c)),
            pl.BlockSpec((None, halo, tc), lambda bi, i, c: (bi, jnp.maximum(i * rpb - 1, 0), cb0 + c)),
            pl.BlockSpec((None, halo, tc), lambda bi, i, c: (bi, jnp.minimum((i + 1) * rpb, last), cb0 + c)),
            pl.BlockSpec((D_CONV, tc), lambda bi, i, c: (0, c)),
            pl.BlockSpec((1, tc), lambda bi, i, c: (0, c)),
        ],
        out_specs=pl.BlockSpec((None, tt, tc), lambda bi, i, c: (bi, i, c)),
        scratch_shapes=[pltpu.VMEM((tt + 2 * halo, tc), F32)],
        compiler_params=_cparams(("parallel", "parallel", "parallel")),
        name="conv_silu",
    )(proj3, proj3, proj3, conv_w, conv_b)


def _cumsum_rows(x):
    row = lax.broadcasted_iota(jnp.int32, x.shape, 0)
    sh = 1
    while sh < x.shape[0]:
        x = x + jnp.where(row >= sh, pltpu.roll(x, sh, 0), 0.0)
        sh *= 2
    return x


def _ssd_direction(xa_ref, dtr_ref, dtb_ref, alog_ref, r_ref, s_ref, y_ref, dskip_ref, *, head0, backward):
    L = CHUNK
    d_ssm = SSM_HEADS * SSM_HEAD_DIM
    n_bc = SSM_GROUPS * SSM_STATE
    dt = jax.nn.softplus(dtr_ref[...] + dtb_ref[...])
    a = dt * (-jnp.exp(alog_ref[...]))
    p = _cumsum_rows(a)
    tot = p[L - 1:L, :]
    ac = (tot - p + a) if backward else p
    eo = jnp.exp(ac)
    ds = jnp.exp(tot - ac)
    cd = jnp.broadcast_to(jnp.exp(tot), (8, LANES))
    stack = jnp.concatenate([dt, eo, ds, cd], axis=0)
    hi = stack.astype(BF16)
    lo = (stack - hi.astype(F32)).astype(BF16)
    rr = r_ref[...]
    ex = _dot(hi, rr) + _dot(lo, rr)
    dtx, eox, dsx, cdx = ex[0:L], ex[L:2 * L], ex[2 * L:3 * L], ex[3 * L:3 * L + 1]

    xs = xa_ref[:, 0:d_ssm].astype(F32)
    xd = xs * dtx
    xdb = xd.astype(BF16)
    wst = (xd * dsx).astype(BF16)
    ac_t = ac.T
    row = lax.broadcasted_iota(jnp.int32, (L, L), 0)
    col = lax.broadcasted_iota(jnp.int32, (L, L), 1)
    mask = (row <= col) if backward else (row >= col)
    lane_lo = col < SSM_HEAD_DIM
    zero_b = jnp.zeros((L, L), BF16)

    for g in range(SSM_GROUPS):
        c0 = g * GROUP_WIDTH
        bg = xa_ref[:, d_ssm + g * SSM_STATE:d_ssm + (g + 1) * SSM_STATE]
        cg = xa_ref[:, d_ssm + n_bc + g * SSM_STATE:d_ssm + n_bc + (g + 1) * SSM_STATE]
        cb = _dot_nt(cg, bg)
        st = s_ref[g]
        y = _dot(cg, st.astype(BF16)) * eox[:, c0:c0 + GROUP_WIDTH]
        if dskip_ref is not None:
            y = y + xs[:, c0:c0 + GROUP_WIDTH] * dskip_ref[:, c0:c0 + GROUP_WIDTH]
        pieces = []
        for pp in range(HEADS_PER_GROUP // 2):
            ms = []
            for e in (2 * pp, 2 * pp + 1):
                hc = head0 + g * HEADS_PER_GROUP + e
                seg = ac[:, hc:hc + 1] - ac_t[hc:hc + 1, :]
                lm = jnp.exp(jnp.where(mask, seg, -jnp.inf))
                ms.append((cb * lm).astype(BF16))
            xpair = xdb[:, c0 + pp * LANES:c0 + (pp + 1) * LANES]
            rhs = jnp.concatenate([jnp.where(lane_lo, xpair, zero_b), jnp.where(lane_lo, zero_b, xpair)], axis=0)
            pieces.append(_dot(jnp.concatenate(ms, axis=1), rhs))
        y = y + jnp.concatenate(pieces, axis=1)
        y_ref[:, c0:c0 + GROUP_WIDTH] = y.astype(BF16)
        bt = bg.astype(F32).T.astype(BF16)
        s_ref[g] = st * cdx[:, c0:c0 + GROUP_WIDTH] + _dot(bt, wst[:, c0:c0 + GROUP_WIDTH])


def _ssd_kernel(dtb_ref, alog_ref, dskip_ref, rf_ref, rb_ref, xaf_ref, dtf_ref, xab_ref, dtbk_ref,
                yf_ref, yb_ref, sf_ref, sb_ref):
    @pl.when(pl.program_id(1) == 0)
    def _():
        sf_ref[...] = jnp.zeros_like(sf_ref)
        sb_ref[...] = jnp.zeros_like(sb_ref)

    _ssd_direction(xaf_ref, dtf_ref, dtb_ref, alog_ref, rf_ref, sf_ref, yf_ref, dskip_ref,
                   head0=0, backward=False)
    _ssd_direction(xab_ref, dtbk_ref, dtb_ref, alog_ref, rb_ref, sb_ref, yb_ref, None,
                   head0=SSM_HEADS, backward=True)


def _ssd(xbc_act, dt_raw3, dtb_row, alog_row, dskip_row, r_f, r_b):
    b, t, wa = xbc_act.shape
    nc = t // CHUNK
    d_ssm = SSM_HEADS * SSM_HEAD_DIM
    const = lambda bi, c: (0, 0)
    return pl.pallas_call(
        _ssd_kernel,
        out_shape=(jax.ShapeDtypeStruct((b, t, d_ssm), BF16), jax.ShapeDtypeStruct((b, t, d_ssm), BF16)),
        grid=(b, nc),
        in_specs=[
            pl.BlockSpec((1, LANES), const),
            pl.BlockSpec((1, LANES), const),
            pl.BlockSpec((1, d_ssm), const),
            pl.BlockSpec((LANES, d_ssm), const),
            pl.BlockSpec((LANES, d_ssm), const),
            pl.BlockSpec((None, CHUNK, wa), lambda bi, c: (bi, c, 0)),
            pl.BlockSpec((None, CHUNK, LANES), lambda bi, c: (bi, c, 0)),
            pl.BlockSpec((None, CHUNK, wa), lambda bi, c: (bi, nc - 1 - c, 0)),
            pl.BlockSpec((None, CHUNK, LANES), lambda bi, c: (bi, nc - 1 - c, 0)),
        ],
        out_specs=(
            pl.BlockSpec((None, CHUNK, d_ssm), lambda bi, c: (bi, c, 0)),
            pl.BlockSpec((None, CHUNK, d_ssm), lambda bi, c: (bi, nc - 1 - c, 0)),
        ),
        scratch_shapes=[
            pltpu.VMEM((SSM_GROUPS, SSM_STATE, GROUP_WIDTH), F32),
            pltpu.VMEM((SSM_GROUPS, SSM_STATE, GROUP_WIDTH), F32),
        ],
        compiler_params=_cparams(("parallel", "arbitrary")),
        name="ssd_scan",
    )(dtb_row, alog_row, dskip_row, r_f, r_b, xbc_act, dt_raw3, xbc_act, dt_raw3)


def _gate_kernel(yf_ref, yb_ref, z_ref, g_ref, o_ref):
    y = (yf_ref[...].astype(F32) + yb_ref[...].astype(F32)) * _silu(z_ref[...].astype(F32))
    for g in range(SSM_GROUPS):
        c0 = g * GROUP_WIDTH
        yg = y[:, c0:c0 + GROUP_WIDTH]
        ms = jnp.mean(yg * yg, axis=-1, keepdims=True)
        o_ref[:, c0:c0 + GROUP_WIDTH] = (yg * lax.rsqrt(ms + EPS) * g_ref[:, c0:c0 + GROUP_WIDTH]).astype(BF16)


def _gate(yf2, yb2, proj2, norm_g, *, z_col0, tm):
    m, d = yf2.shape
    zb = z_col0 // d
    return pl.pallas_call(
        _gate_kernel,
        out_shape=jax.ShapeDtypeStruct((m, d), BF16),
        grid=(m // tm,),
        in_specs=[
            pl.BlockSpec((tm, d), lambda i: (i, 0)),
            pl.BlockSpec((tm, d), lambda i: (i, 0)),
            pl.BlockSpec((tm, d), lambda i: (i, zb)),
            pl.BlockSpec((1, d), lambda i: (0, 0)),
        ],
        out_specs=pl.BlockSpec((tm, d), lambda i: (i, 0)),
        compiler_params=_cparams(("parallel",)),
        name="ssm_gate_norm",
    )(yf2, yb2, proj2, norm_g)


def _mem_kv_kernel(mem_ref, w_ref, o_ref):
    o_ref[...] = _dot(mem_ref[...].astype(BF16), w_ref[...]).astype(BF16)


def _mem_kv(mem, w_kv):
    b, mt, d = mem.shape
    n = w_kv.shape[1]
    return pl.pallas_call(
        _mem_kv_kernel,
        out_shape=jax.ShapeDtypeStruct((b, mt, n), BF16),
        grid=(b,),
        in_specs=[
            pl.BlockSpec((None, mt, d), lambda bi: (bi, 0, 0)),
            pl.BlockSpec((d, n), lambda bi: (0, 0)),
        ],
        out_specs=pl.BlockSpec((None, mt, n), lambda bi: (bi, 0, 0)),
        compiler_params=_cparams(("parallel",)),
        name="mem_kv_proj",
    )(mem, w_kv)


def _mem_attn_kernel(q_ref, g_ref, kv_ref, o_ref):
    d = MEM_HEAD_DIM
    d_mem = MEM_HEADS * d
    scale = 1.0 / math.sqrt(d)
    for hh in range(MEM_HEADS):
        q = q_ref[:, hh * d:(hh + 1) * d]
        k = kv_ref[:, hh * d:(hh + 1) * d]
        v = kv_ref[:, d_mem + hh * d:d_mem + (hh + 1) * d]
        s = _dot_nt(q, k) * scale
        p = jnp.exp(s - jnp.max(s, axis=-1, keepdims=True))
        o = _dot(p.astype(BF16), v) / jnp.sum(p, axis=-1, keepdims=True)
        o_ref[:, hh * d:(hh + 1) * d] = (o * _silu(g_ref[:, hh * d:(hh + 1) * d].astype(F32))).astype(BF16)


def _mem_attn(proj3, kv, *, q_col0, tq):
    b, t, _ = proj3.shape
    mt, n = kv.shape[1:]
    d_mem = MEM_HEADS * MEM_HEAD_DIM
    qb = q_col0 // d_mem
    return pl.pallas_call(
        _mem_attn_kernel,
        out_shape=jax.ShapeDtypeStruct((b, t, d_mem), BF16),
        grid=(b, t // tq),
        in_specs=[
            pl.BlockSpec((None, tq, d_mem), lambda bi, i: (bi, i, qb)),
            pl.BlockSpec((None, tq, d_mem), lambda bi, i: (bi, i, qb + 1)),
            pl.BlockSpec((None, mt, n), lambda bi, i: (bi, 0, 0)),
        ],
        out_specs=pl.BlockSpec((None, tq, d_mem), lambda bi, i: (bi, i, 0)),
        compiler_params=_cparams(("parallel", "parallel")),
        name="mem_attn",
    )(proj3, proj3, kv)


def _out_proj_kernel(x_ref, gi_ref, bi_ref, ha_ref, hs_ref, hm_ref, w_ref, g_ref, b_ref, o_ref):
    da = ha_ref.shape[1]
    dsm = hs_ref.shape[1]
    xn = _layer_norm(x_ref[...], gi_ref[...], bi_ref[...])
    out = _dot(ha_ref[...], w_ref[0:da, :])
    out = out + _dot(hs_ref[...], w_ref[da:da + dsm, :])
    out = out + _dot(hm_ref[...], w_ref[da + dsm:, :])
    o_ref[...] = _layer_norm(ALPHA * xn + out, g_ref[...], b_ref[...])


def _out_proj(x2, ln_in_g, ln_in_b, h_att, h_ssm, h_mem, w_out, ln_g, ln_b, *, tm):
    m, d = x2.shape
    da, dsm, dm = h_att.shape[1], h_ssm.shape[1], h_mem.shape[1]
    row = lambda i: (i, 0)
    const = lambda i: (0, 0)
    return pl.pallas_call(
        _out_proj_kernel,
        out_shape=jax.ShapeDtypeStruct((m, d), F32),
        grid=(m // tm,),
        in_specs=[
            pl.BlockSpec((tm, d), row),
            pl.BlockSpec((1, d), const),
            pl.BlockSpec((1, d), const),
            pl.BlockSpec((tm, da), row),
            pl.BlockSpec((tm, dsm), row),
            pl.BlockSpec((tm, dm), row),
            pl.BlockSpec(w_out.shape, const, pipeline_mode=pl.Buffered(1)),
            pl.BlockSpec((1, d), const),
            pl.BlockSpec((1, d), const),
        ],
        out_specs=pl.BlockSpec((tm, d), row),
        compiler_params=_cparams(("parallel",)),
        name="out_proj_ln",
    )(x2, ln_in_g, ln_in_b, h_att, h_ssm, h_mem, w_out, ln_g, ln_b)


def _tile(n, pref):
    t = min(n, pref)
    while n % t:
        t //= 2
    return t


def _trunk(x, mem, p):
    b, t, d = x.shape
    m = b * t
    x2 = x.reshape(m, d)
    proj2, dt_raw = _in_proj(x2, p["ln_in_g"], p["ln_in_b"], p["w_main"], p["w_dt"],
                             tm=_tile(m, 1024), tn=512)
    n = proj2.shape[1]
    proj3 = proj2.reshape(b, t, n)
    d_att = ATT_HEADS * ATT_V_DIM
    d_ssm = SSM_HEADS * SSM_HEAD_DIM

    h_att = _attention(proj3, p["diff_lambda"], p["subln_g"], lam_init=p["lam_init"],
                       tq=_tile(t, 512), tk=_tile(t, 512))

    xbc_act = _conv(proj3, p["conv_w"], p["conv_b"], col0=4 * d_att + d_ssm, tt=_tile(t, 512), tc=1024)
    y_f, y_b = _ssd(xbc_act, dt_raw.reshape(b, t, LANES), p["dtb_row"], p["alog_row"], p["dskip_row"],
                    p["r_f"], p["r_b"])
    h_ssm = _gate(y_f.reshape(m, d_ssm), y_b.reshape(m, d_ssm), proj2, p["ssm_norm_g"],
                  z_col0=4 * d_att, tm=_tile(m, 512))

    kv = _mem_kv(mem, p["w_mem_kv"])
    h_mem = _mem_attn(proj3, kv, q_col0=4 * d_att + d_ssm + p["conv_w"].shape[1], tq=_tile(t, 512))

    y = _out_proj(x2, p["ln_in_g"], p["ln_in_b"], h_att.reshape(m, d_att), h_ssm,
                  h_mem.reshape(m, MEM_HEADS * MEM_HEAD_DIM), p["w_out"], p["ln_g"], p["ln_b"],
                  tm=_tile(m, 512))
    return y.reshape(b, t, d)


def _expansion_matrix(head0):
    rows = lax.broadcasted_iota(jnp.int32, (LANES, SSM_HEADS * SSM_HEAD_DIM), 0)
    cols = lax.broadcasted_iota(jnp.int32, (LANES, SSM_HEADS * SSM_HEAD_DIM), 1)
    return (rows == head0 + cols // SSM_HEAD_DIM).astype(BF16)


def kernel(x_prompt, x_sample, mem_prompt, mem_sample, ln_in_g, ln_in_b, w_in, conv_w, conv_b, dt_bias, a_log, d_skip, ssm_norm_g, diff_lambda, subln_g, w_mem_kv, w_out, ln_g, ln_b):
    assert w_in.shape[0] == DEPTH
    d_att = ATT_HEADS * ATT_V_DIM
    d_ssm = SSM_HEADS * SSM_HEAD_DIM
    d_xbc = conv_w.shape[-1]
    dt0 = 4 * d_att + d_ssm + d_xbc
    dt1 = dt0 + 2 * SSM_HEADS
    row = lambda v: v.reshape(1, -1).astype(F32)
    pad_heads = lambda v: jnp.pad(v.reshape(1, -1).astype(F32), ((0, 0), (0, LANES - 2 * SSM_HEADS)))

    l = 0
    w = w_in[l]
    p = {
        "ln_in_g": row(ln_in_g), "ln_in_b": row(ln_in_b),
        "w_main": jnp.concatenate([w[:, :dt0], w[:, dt1:]], axis=1).astype(BF16),
        "w_dt": jnp.pad(w[:, dt0:dt1], ((0, 0), (0, LANES - 2 * SSM_HEADS))).astype(BF16),
        "conv_w": conv_w[l].astype(F32), "conv_b": row(conv_b[l]),
        "dtb_row": pad_heads(dt_bias[l]), "alog_row": pad_heads(a_log[l]),
        "dskip_row": row(jnp.repeat(d_skip[l], SSM_HEAD_DIM)),
        "r_f": _expansion_matrix(0), "r_b": _expansion_matrix(SSM_HEADS),
        "ssm_norm_g": row(ssm_norm_g[l]),
        "diff_lambda": diff_lambda[l].astype(F32), "subln_g": row(subln_g[l]),
        "lam_init": 0.8 - 0.6 * math.exp(-0.3 * l),
        "w_mem_kv": w_mem_kv[l].astype(BF16), "w_out": w_out[l].astype(BF16),
        "ln_g": row(ln_g[l]), "ln_b": row(ln_b[l]),
    }
    return (_trunk(x_prompt, mem_prompt, p), _trunk(x_sample, mem_sample, p))
```

```python
import functools
import math

import jax
import jax.numpy as jnp
from jax import lax
from jax.experimental import pallas as pl
from jax.experimental.pallas import tpu as pltpu

F32 = jnp.float32
BF16 = jnp.bfloat16

DEPTH = 1
EPS = 1e-5
ATT_HEADS = 8
ATT_HEAD_DIM = 128
ATT_V_DIM = 2 * ATT_HEAD_DIM
SSM_HEADS = 32
SSM_HEAD_DIM = 64
SSM_GROUPS = 4
SSM_STATE = 128
HEADS_PER_GROUP = SSM_HEADS // SSM_GROUPS
GROUP_WIDTH = HEADS_PER_GROUP * SSM_HEAD_DIM
D_CONV = 5
CHUNK = 128
MEM_HEADS = 4
MEM_HEAD_DIM = 128
LANES = 128
LOG2E = 1.4426950408889634
CONV_HALO_ROWS = 16
ALPHA = (2.0 * DEPTH) ** 0.25
VMEM_LIMIT_BYTES = 56 * 1024 * 1024


def _cparams(sem):
    return pltpu.CompilerParams(dimension_semantics=sem, vmem_limit_bytes=VMEM_LIMIT_BYTES)


def _layer_norm(x, g, b):
    mu = jnp.mean(x, axis=-1, keepdims=True)
    xc = x - mu
    var = jnp.mean(xc * xc, axis=-1, keepdims=True)
    return xc * lax.rsqrt(var + EPS) * g + b


def _silu(x):
    return x * jax.nn.sigmoid(x)


def _dot(a, b):
    return jnp.dot(a, b, preferred_element_type=F32)


def _dot_nt(a, b):
    return lax.dot_general(a, b, (((1,), (1,)), ((), ())), preferred_element_type=F32)


def _in_proj_kernel(x_ref, g_ref, b_ref, w_ref, wdt_ref, proj_ref, dt_ref, xn_ref):
    @pl.when(pl.program_id(1) == 0)
    def _():
        xn = _layer_norm(x_ref[...], g_ref[...], b_ref[...]).astype(BF16)
        xn_ref[...] = xn
        dt_ref[...] = _dot(xn, wdt_ref[...])

    proj_ref[...] = _dot(xn_ref[...], w_ref[...]).astype(BF16)


def _in_proj(x2, ln_g, ln_b, w_main, w_dt, *, tm, tn):
    m, d = x2.shape
    n = w_main.shape[1]
    return pl.pallas_call(
        _in_proj_kernel,
        out_shape=(jax.ShapeDtypeStruct((m, n), BF16), jax.ShapeDtypeStruct((m, LANES), F32)),
        grid=(m // tm, n // tn),
        in_specs=[
            pl.BlockSpec((tm, d), lambda i, j: (i, 0)),
            pl.BlockSpec((1, d), lambda i, j: (0, 0)),
            pl.BlockSpec((1, d), lambda i, j: (0, 0)),
            pl.BlockSpec((d, tn), lambda i, j: (0, j)),
            pl.BlockSpec((d, LANES), lambda i, j: (0, 0)),
        ],
        out_specs=(
            pl.BlockSpec((tm, tn), lambda i, j: (i, j)),
            pl.BlockSpec((tm, LANES), lambda i, j: (i, 0)),
        ),
        scratch_shapes=[pltpu.VMEM((tm, d), BF16)],
        compiler_params=_cparams(("parallel", "arbitrary")),
        name="in_proj",
    )(x2, ln_g, ln_b, w_main, w_dt)


def _attn_kernel(dl_ref, subg_ref, q_ref, k_ref, v_ref, g_ref, o_ref, acc_ref, qt_ref, vt_ref,
                 *, blk, lam_init):
    h = pl.program_id(1)
    i = pl.program_id(2)
    nblk = k_ref.shape[0] // blk
    d = ATT_HEAD_DIM
    c = jnp.exp2(-(h + 1).astype(F32)) * LOG2E

    @pl.when(i == 0)
    def _():
        def tr(jj, carry):
            start = pl.multiple_of(jj * blk, blk)
            vt_ref[jj] = v_ref[pl.ds(start, blk), :].astype(F32).T.astype(BF16)
            return carry
        lax.fori_loop(0, nblk, tr, 0)

    def split3(x):
        hi = x.astype(BF16).astype(F32)
        mid = (x - hi).astype(BF16).astype(F32)
        return hi, mid, x - hi - mid

    lane = lax.broadcasted_iota(jnp.int32, (blk, d), 1)
    ku = split3(c * lax.broadcasted_iota(jnp.int32, (blk, d), 0).astype(F32))
    aug_k = jnp.where(lane < 3, -1.0, jnp.where(lane == 3, ku[0], jnp.where(lane == 4, ku[1],
                      jnp.where(lane == 5, ku[2], 0.0)))).astype(BF16)
    row = lax.broadcasted_iota(jnp.int32, (d, blk), 0)
    qr = split3(c * lax.broadcasted_iota(jnp.int32, (d, blk), 1).astype(F32))
    aug_qt = jnp.where(row == 0, qr[0], jnp.where(row == 1, qr[1], jnp.where(row == 2, qr[2],
                       jnp.where(row < 6, 1.0, 0.0))))
    q = q_ref[...].astype(F32) * (LOG2E / math.sqrt(d))
    for mp in range(2):
        qt = q[:, mp * d:(mp + 1) * d].T.astype(BF16)
        qt_ref[mp] = jnp.concatenate([qt, aug_qt.astype(BF16)], axis=0)
        qt_ref[2 + mp] = jnp.concatenate([qt, (-aug_qt).astype(BF16)], axis=0)

    acc_ref[...] = jnp.zeros_like(acc_ref)
    qw = min(blk, 2 * LANES)
    nq = blk // qw

    def step(j, carry, variant, delta, corr):
        start = pl.multiple_of(j * blk, blk)
        k_blk = k_ref[pl.ds(start, blk), :]
        vt = vt_ref[j]
        kp = [jnp.concatenate([k_blk[:, mp * d:(mp + 1) * d], aug_k], axis=1) for mp in range(2)]
        chains = [(mp, hf) for mp in range(2) for hf in range(nq)]

        def scores(ch):
            mp, hf = ch
            st = _dot(kp[mp], qt_ref[2 * variant + mp, :, hf * qw:(hf + 1) * qw])
            return st if corr is None else st + corr[:, hf * qw:(hf + 1) * qw]

        def finish(ch, st):
            mp, hf = ch
            m_old, l_old = carry[2 * mp][:, hf * qw:(hf + 1) * qw], carry[2 * mp + 1][:, hf * qw:(hf + 1) * qw]
            m_new = jnp.maximum(m_old, jnp.max(st, axis=0, keepdims=True) + delta)
            a = jnp.exp2(m_old - m_new)
            pt = jnp.exp2(st - (m_new - delta))
            l_new = a * l_old + jnp.sum(pt, axis=0, keepdims=True)
            acc_ref[mp, :, hf * qw:(hf + 1) * qw] = (a * acc_ref[mp, :, hf * qw:(hf + 1) * qw]
                                                     + _dot(vt, pt.astype(BF16)))
            return m_new, l_new

        stats = {}
        pending = scores(chains[0])
        for n, ch in enumerate(chains):
            st = pending
            if n + 1 < len(chains):
                pending = scores(chains[n + 1])
            stats[ch] = finish(ch, st)
        new = []
        for mp in range(2):
            new.append(jnp.concatenate([stats[(mp, hf)][0] for hf in range(nq)], axis=1))
            new.append(jnp.concatenate([stats[(mp, hf)][1] for hf in range(nq)], axis=1))
        return tuple(new)

    def below(j, carry):
        return step(j, carry, 0, -c * ((i - j) * blk).astype(F32), None)

    def above(j, carry):
        return step(j, carry, 1, -c * ((j - i) * blk).astype(F32), None)

    neg = jnp.full((1, blk), -1e30, F32)
    zero = jnp.zeros((1, blk), F32)
    carry = lax.fori_loop(0, i, below, (neg, zero, neg, zero))
    ur = (lax.broadcasted_iota(jnp.int32, (blk, blk), 0) - lax.broadcasted_iota(jnp.int32, (blk, blk), 1))
    carry = step(i, carry, 0, 0.0, (-2.0 * c) * jnp.maximum(ur, 0).astype(F32))
    _, l0, _, l1 = lax.fori_loop(i + 1, nblk, above, carry)

    dl = dl_ref[...]
    lam = (jnp.exp(jnp.sum(dl[0:1] * dl[1:2], axis=-1, keepdims=True))
           - jnp.exp(jnp.sum(dl[2:3] * dl[3:4], axis=-1, keepdims=True)) + lam_init)
    out = (acc_ref[0] / l0 - lam * (acc_ref[1] / l1)).T
    ms = jnp.mean(out * out, axis=-1, keepdims=True)
    out = out * lax.rsqrt(ms + EPS) * subg_ref[...] * (1.0 - lam_init)
    o_ref[...] = (out * _silu(g_ref[...].astype(F32))).astype(BF16)


def _attention(proj3, diff_lambda, subln_g, *, lam_init, blk):
    b, t, _ = proj3.shape
    hh = ATT_HEADS
    tq = blk
    kern = functools.partial(_attn_kernel, blk=blk, lam_init=lam_init)
    return pl.pallas_call(
        kern,
        out_shape=jax.ShapeDtypeStruct((b, t, hh * ATT_V_DIM), BF16),
        grid=(b, hh, t // tq),
        in_specs=[
            pl.BlockSpec((4, ATT_HEAD_DIM), lambda bi, h, i: (0, 0)),
            pl.BlockSpec((1, ATT_V_DIM), lambda bi, h, i: (0, 0)),
            pl.BlockSpec((None, tq, ATT_V_DIM), lambda bi, h, i: (bi, i, h)),
            pl.BlockSpec((None, t, ATT_V_DIM), lambda bi, h, i: (bi, 0, hh + h)),
            pl.BlockSpec((None, t, ATT_V_DIM), lambda bi, h, i: (bi, 0, 2 * hh + h)),
            pl.BlockSpec((None, tq, ATT_V_DIM), lambda bi, h, i: (bi, i, 3 * hh + h)),
        ],
        out_specs=pl.BlockSpec((None, tq, ATT_V_DIM), lambda bi, h, i: (bi, i, h)),
        scratch_shapes=[
            pltpu.VMEM((2, ATT_V_DIM, blk), F32),
            pltpu.VMEM((4, 2 * ATT_HEAD_DIM, blk), BF16),
            pltpu.VMEM((t // blk, ATT_V_DIM, blk), BF16),
        ],
        compiler_params=_cparams(("parallel", "parallel", "arbitrary")),
        name="diff_attn",
    )(diff_lambda, subln_g, proj3, proj3, proj3, proj3)


def _conv_kernel(cur_ref, prev_ref, next_ref, w_ref, b_ref, o_ref, u_ref, *, tt):
    i = pl.program_id(1)
    halo = CONV_HALO_ROWS
    pad = (D_CONV - 1) // 2
    prev = prev_ref[...].astype(F32)
    nxt = next_ref[...].astype(F32)
    u_ref[0:halo, :] = jnp.where(i > 0, prev, 0.0)
    u_ref[halo:halo + tt, :] = cur_ref[...].astype(F32)
    u_ref[halo + tt:2 * halo + tt, :] = jnp.where(i < pl.num_programs(1) - 1, nxt, 0.0)
    acc = u_ref[pl.ds(halo - pad, tt), :] * w_ref[0:1, :]
    for k in range(1, D_CONV):
        acc = acc + u_ref[pl.ds(halo - pad + k, tt), :] * w_ref[k:k + 1, :]
    o_ref[...] = _silu(acc + b_ref[...]).astype(BF16)


def _conv(proj3, conv_w, conv_b, *, col0, tt, tc):
    b, t, _ = proj3.shape
    width = conv_w.shape[1]
    halo = CONV_HALO_ROWS
    cb0 = col0 // tc
    rpb = tt // halo
    last = t // halo - 1
    kern = functools.partial(_conv_kernel, tt=tt)
    return pl.pallas_call(
        kern,
        out_shape=jax.ShapeDtypeStruct((b, t, width), BF16),
        grid=(b, t // tt, width // tc),
        in_specs=[
            pl.BlockSpec((None, tt, tc), lambda bi, i, c: (bi, i, cb0 + c)),
            pl.BlockSpec((None, halo, tc), lambda bi, i, c: (bi, jnp.maximum(i * rpb - 1, 0), cb0 + c)),
            pl.BlockSpec((None, halo, tc), lambda bi, i, c: (bi, jnp.minimum((i + 1) * rpb, last), cb0 + c)),
            pl.BlockSpec((D_CONV, tc), lambda bi, i, c: (0, c)),
            pl.BlockSpec((1, tc), lambda bi, i, c: (0, c)),
        ],
        out_specs=pl.BlockSpec((None, tt, tc), lambda bi, i, c: (bi, i, c)),
        scratch_shapes=[pltpu.VMEM((tt + 2 * halo, tc), F32)],
        compiler_params=_cparams(("parallel", "parallel", "parallel")),
        name="conv_silu",
    )(proj3, proj3, proj3, conv_w, conv_b)


def _cumsum_rows(x):
    row = lax.broadcasted_iota(jnp.int32, x.shape, 0)
    sh = 1
    while sh < x.shape[0]:
        x = x + jnp.where(row >= sh, pltpu.roll(x, sh, 0), 0.0)
        sh *= 2
    return x


def _ssd_direction(xa_ref, dtr_ref, dtb_ref, alog_ref, r_ref, s_ref, y_ref, dskip_ref, *, head0, backward):
    L = CHUNK
    d_ssm = SSM_HEADS * SSM_HEAD_DIM
    n_bc = SSM_GROUPS * SSM_STATE
    dt = jax.nn.softplus(dtr_ref[...] + dtb_ref[...])
    a = dt * (-jnp.exp(alog_ref[...]))
    p = _cumsum_rows(a)
    tot = p[L - 1:L, :]
    ac = (tot - p + a) if backward else p
    eo = jnp.exp(ac)
    ds = jnp.exp(tot - ac)
    cd = jnp.broadcast_to(jnp.exp(tot), (8, LANES))
    stack = jnp.concatenate([dt, eo, ds, cd], axis=0)
    hi = stack.astype(BF16)
    lo = (stack - hi.astype(F32)).astype(BF16)
    rr = r_ref[...]
    ex = _dot(hi, rr) + _dot(lo, rr)
    dtx, eox, dsx, cdx = ex[0:L], ex[L:2 * L], ex[2 * L:3 * L], ex[3 * L:3 * L + 1]

    xs = xa_ref[:, 0:d_ssm].astype(F32)
    xd = xs * dtx
    xdb = xd.astype(BF16)
    wst = (xd * dsx).astype(BF16)
    ac_t = ac.T
    row = lax.broadcasted_iota(jnp.int32, (L, L), 0)
    col = lax.broadcasted_iota(jnp.int32, (L, L), 1)
    mask = (row <= col) if backward else (row >= col)
    lane_lo = col < SSM_HEAD_DIM
    zero_b = jnp.zeros((L, L), BF16)

    for g in range(SSM_GROUPS):
        c0 = g * GROUP_WIDTH
        bg = xa_ref[:, d_ssm + g * SSM_STATE:d_ssm + (g + 1) * SSM_STATE]
        cg = xa_ref[:, d_ssm + n_bc + g * SSM_STATE:d_ssm + n_bc + (g + 1) * SSM_STATE]
        cb = _dot_nt(cg, bg)
        st = s_ref[g]
        y = _dot(cg, st.astype(BF16)) * eox[:, c0:c0 + GROUP_WIDTH]
        if dskip_ref is not None:
            y = y + xs[:, c0:c0 + GROUP_WIDTH] * dskip_ref[:, c0:c0 + GROUP_WIDTH]
        pieces = []
        for pp in range(HEADS_PER_GROUP // 2):
            ms = []
            for e in (2 * pp, 2 * pp + 1):
                hc = head0 + g * HEADS_PER_GROUP + e
                seg = ac[:, hc:hc + 1] - ac_t[hc:hc + 1, :]
                lm = jnp.exp(jnp.where(mask, seg, -jnp.inf))
                ms.append((cb * lm).astype(BF16))
            xpair = xdb[:, c0 + pp * LANES:c0 + (pp + 1) * LANES]
            rhs = jnp.concatenate([jnp.where(lane_lo, xpair, zero_b), jnp.where(lane_lo, zero_b, xpair)], axis=0)
            pieces.append(_dot(jnp.concatenate(ms, axis=1), rhs))
        y = y + jnp.concatenate(pieces, axis=1)
        y_ref[:, c0:c0 + GROUP_WIDTH] = y.astype(BF16)
        bt = bg.astype(F32).T.astype(BF16)
        s_ref[g] = st * cdx[:, c0:c0 + GROUP_WIDTH] + _dot(bt, wst[:, c0:c0 + GROUP_WIDTH])


def _ssd_kernel(dtb_ref, alog_ref, dskip_ref, rf_ref, rb_ref, xaf_ref, dtf_ref, xab_ref, dtbk_ref,
                yf_ref, yb_ref, sf_ref, sb_ref):
    @pl.when(pl.program_id(1) == 0)
    def _():
        sf_ref[...] = jnp.zeros_like(sf_ref)
        sb_ref[...] = jnp.zeros_like(sb_ref)

    _ssd_direction(xaf_ref, dtf_ref, dtb_ref, alog_ref, rf_ref, sf_ref, yf_ref, dskip_ref,
                   head0=0, backward=False)
    _ssd_direction(xab_ref, dtbk_ref, dtb_ref, alog_ref, rb_ref, sb_ref, yb_ref, None,
                   head0=SSM_HEADS, backward=True)


def _ssd(xbc_act, dt_raw3, dtb_row, alog_row, dskip_row, r_f, r_b):
    b, t, wa = xbc_act.shape
    nc = t // CHUNK
    d_ssm = SSM_HEADS * SSM_HEAD_DIM
    const = lambda bi, c: (0, 0)
    return pl.pallas_call(
        _ssd_kernel,
        out_shape=(jax.ShapeDtypeStruct((b, t, d_ssm), BF16), jax.ShapeDtypeStruct((b, t, d_ssm), BF16)),
        grid=(b, nc),
        in_specs=[
            pl.BlockSpec((1, LANES), const),
            pl.BlockSpec((1, LANES), const),
            pl.BlockSpec((1, d_ssm), const),
            pl.BlockSpec((LANES, d_ssm), const),
            pl.BlockSpec((LANES, d_ssm), const),
            pl.BlockSpec((None, CHUNK, wa), lambda bi, c: (bi, c, 0)),
            pl.BlockSpec((None, CHUNK, LANES), lambda bi, c: (bi, c, 0)),
            pl.BlockSpec((None, CHUNK, wa), lambda bi, c: (bi, nc - 1 - c, 0)),
            pl.BlockSpec((None, CHUNK, LANES), lambda bi, c: (bi, nc - 1 - c, 0)),
        ],
        out_specs=(
            pl.BlockSpec((None, CHUNK, d_ssm), lambda bi, c: (bi, c, 0)),
            pl.BlockSpec((None, CHUNK, d_ssm), lambda bi, c: (bi, nc - 1 - c, 0)),
        ),
        scratch_shapes=[
            pltpu.VMEM((SSM_GROUPS, SSM_STATE, GROUP_WIDTH), F32),
            pltpu.VMEM((SSM_GROUPS, SSM_STATE, GROUP_WIDTH), F32),
        ],
        compiler_params=_cparams(("parallel", "arbitrary")),
        name="ssd_scan",
    )(dtb_row, alog_row, dskip_row, r_f, r_b, xbc_act, dt_raw3, xbc_act, dt_raw3)


def _gate_kernel(yf_ref, yb_ref, z_ref, g_ref, o_ref):
    y = (yf_ref[...].astype(F32) + yb_ref[...].astype(F32)) * _silu(z_ref[...].astype(F32))
    for g in range(SSM_GROUPS):
        c0 = g * GROUP_WIDTH
        yg = y[:, c0:c0 + GROUP_WIDTH]
        ms = jnp.mean(yg * yg, axis=-1, keepdims=True)
        o_ref[:, c0:c0 + GROUP_WIDTH] = (yg * lax.rsqrt(ms + EPS) * g_ref[:, c0:c0 + GROUP_WIDTH]).astype(BF16)


def _gate(yf2, yb2, proj2, norm_g, *, z_col0, tm):
    m, d = yf2.shape
    zb = z_col0 // d
    return pl.pallas_call(
        _gate_kernel,
        out_shape=jax.ShapeDtypeStruct((m, d), BF16),
        grid=(m // tm,),
        in_specs=[
            pl.BlockSpec((tm, d), lambda i: (i, 0)),
            pl.BlockSpec((tm, d), lambda i: (i, 0)),
            pl.BlockSpec((tm, d), lambda i: (i, zb)),
            pl.BlockSpec((1, d), lambda i: (0, 0)),
        ],
        out_specs=pl.BlockSpec((tm, d), lambda i: (i, 0)),
        compiler_params=_cparams(("parallel",)),
        name="ssm_gate_norm",
    )(yf2, yb2, proj2, norm_g)


def _mem_kv_kernel(mem_ref, w_ref, o_ref):
    o_ref[...] = _dot(mem_ref[...].astype(BF16), w_ref[...]).astype(BF16)


def _mem_kv(mem, w_kv):
    b, mt, d = mem.shape
    n = w_kv.shape[1]
    return pl.pallas_call(
        _mem_kv_kernel,
        out_shape=jax.ShapeDtypeStruct((b, mt, n), BF16),
        grid=(b,),
        in_specs=[
            pl.BlockSpec((None, mt, d), lambda bi: (bi, 0, 0)),
            pl.BlockSpec((d, n), lambda bi: (0, 0)),
        ],
        out_specs=pl.BlockSpec((None, mt, n), lambda bi: (bi, 0, 0)),
        compiler_params=_cparams(("parallel",)),
        name="mem_kv_proj",
    )(mem, w_kv)


def _mem_attn_kernel(q_ref, g_ref, kv_ref, o_ref):
    d = MEM_HEAD_DIM
    d_mem = MEM_HEADS * d
    scale = 1.0 / math.sqrt(d)
    for hh in range(MEM_HEADS):
        q = q_ref[:, hh * d:(hh + 1) * d]
        k = kv_ref[:, hh * d:(hh + 1) * d]
        v = kv_ref[:, d_mem + hh * d:d_mem + (hh + 1) * d]
        s = _dot_nt(q, k) * scale
        p = jnp.exp(s - jnp.max(s, axis=-1, keepdims=True))
        o = _dot(p.astype(BF16), v) / jnp.sum(p, axis=-1, keepdims=True)
        o_ref[:, hh * d:(hh + 1) * d] = (o * _silu(g_ref[:, hh * d:(hh + 1) * d].astype(F32))).astype(BF16)


def _mem_attn(proj3, kv, *, q_col0, tq):
    b, t, _ = proj3.shape
    mt, n = kv.shape[1:]
    d_mem = MEM_HEADS * MEM_HEAD_DIM
    qb = q_col0 // d_mem
    return pl.pallas_call(
        _mem_attn_kernel,
        out_shape=jax.ShapeDtypeStruct((b, t, d_mem), BF16),
        grid=(b, t // tq),
        in_specs=[
            pl.BlockSpec((None, tq, d_mem), lambda bi, i: (bi, i, qb)),
            pl.BlockSpec((None, tq, d_mem), lambda bi, i: (bi, i, qb + 1)),
            pl.BlockSpec((None, mt, n), lambda bi, i: (bi, 0, 0)),
        ],
        out_specs=pl.BlockSpec((None, tq, d_mem), lambda bi, i: (bi, i, 0)),
        compiler_params=_cparams(("parallel", "parallel")),
        name="mem_attn",
    )(proj3, proj3, kv)


def _out_proj_kernel(x_ref, gi_ref, bi_ref, ha_ref, hs_ref, hm_ref, w_ref, g_ref, b_ref, o_ref):
    da = ha_ref.shape[1]
    dsm = hs_ref.shape[1]
    xn = _layer_norm(x_ref[...], gi_ref[...], bi_ref[...])
    out = _dot(ha_ref[...], w_ref[0:da, :])
    out = out + _dot(hs_ref[...], w_ref[da:da + dsm, :])
    out = out + _dot(hm_ref[...], w_ref[da + dsm:, :])
    o_ref[...] = _layer_norm(ALPHA * xn + out, g_ref[...], b_ref[...])


def _out_proj(x2, ln_in_g, ln_in_b, h_att, h_ssm, h_mem, w_out, ln_g, ln_b, *, tm):
    m, d = x2.shape
    da, dsm, dm = h_att.shape[1], h_ssm.shape[1], h_mem.shape[1]
    row = lambda i: (i, 0)
    const = lambda i: (0, 0)
    return pl.pallas_call(
        _out_proj_kernel,
        out_shape=jax.ShapeDtypeStruct((m, d), F32),
        grid=(m // tm,),
        in_specs=[
            pl.BlockSpec((tm, d), row),
            pl.BlockSpec((1, d), const),
            pl.BlockSpec((1, d), const),
            pl.BlockSpec((tm, da), row),
            pl.BlockSpec((tm, dsm), row),
            pl.BlockSpec((tm, dm), row),
            pl.BlockSpec(w_out.shape, const, pipeline_mode=pl.Buffered(1)),
            pl.BlockSpec((1, d), const),
            pl.BlockSpec((1, d), const),
        ],
        out_specs=pl.BlockSpec((tm, d), row),
        compiler_params=_cparams(("parallel",)),
        name="out_proj_ln",
    )(x2, ln_in_g, ln_in_b, h_att, h_ssm, h_mem, w_out, ln_g, ln_b)


def _tile(n, pref):
    t = min(n, pref)
    while n % t:
        t //= 2
    return t


def _trunk(x, mem, p):
    b, t, d = x.shape
    m = b * t
    x2 = x.reshape(m, d)
    proj2, dt_raw = _in_proj(x2, p["ln_in_g"], p["ln_in_b"], p["w_main"], p["w_dt"],
                             tm=_tile(m, 1024), tn=512)
    n = proj2.shape[1]
    proj3 = proj2.reshape(b, t, n)
    d_att = ATT_HEADS * ATT_V_DIM
    d_ssm = SSM_HEADS * SSM_HEAD_DIM

    h_att = _attention(proj3, p["diff_lambda"], p["subln_g"], lam_init=p["lam_init"],
                       blk=_tile(t, 512))

    xbc_act = _conv(proj3, p["conv_w"], p["conv_b"], col0=4 * d_att + d_ssm, tt=_tile(t, 512), tc=1024)
    y_f, y_b = _ssd(xbc_act, dt_raw.reshape(b, t, LANES), p["dtb_row"], p["alog_row"], p["dskip_row"],
                    p["r_f"], p["r_b"])
    h_ssm = _gate(y_f.reshape(m, d_ssm), y_b.reshape(m, d_ssm), proj2, p["ssm_norm_g"],
                  z_col0=4 * d_att, tm=_tile(m, 512))

    kv = _mem_kv(mem, p["w_mem_kv"])
    h_mem = _mem_attn(proj3, kv, q_col0=4 * d_att + d_ssm + p["conv_w"].shape[1], tq=_tile(t, 512))

    y = _out_proj(x2, p["ln_in_g"], p["ln_in_b"], h_att.reshape(m, d_att), h_ssm,
                  h_mem.reshape(m, MEM_HEADS * MEM_HEAD_DIM), p["w_out"], p["ln_g"], p["ln_b"],
                  tm=_tile(m, 512))
    return y.reshape(b, t, d)


def _expansion_matrix(head0):
    rows = lax.broadcasted_iota(jnp.int32, (LANES, SSM_HEADS * SSM_HEAD_DIM), 0)
    cols = lax.broadcasted_iota(jnp.int32, (LANES, SSM_HEADS * SSM_HEAD_DIM), 1)
    return (rows == head0 + cols // SSM_HEAD_DIM).astype(BF16)


def kernel(x_prompt, x_sample, mem_prompt, mem_sample, ln_in_g, ln_in_b, w_in, conv_w, conv_b, dt_bias, a_log, d_skip, ssm_norm_g, diff_lambda, subln_g, w_mem_kv, w_out, ln_g, ln_b):
    assert w_in.shape[0] == DEPTH
    d_att = ATT_HEADS * ATT_V_DIM
    d_ssm = SSM_HEADS * SSM_HEAD_DIM
    d_xbc = conv_w.shape[-1]
    dt0 = 4 * d_att + d_ssm + d_xbc
    dt1 = dt0 + 2 * SSM_HEADS
    row = lambda v: v.reshape(1, -1).astype(F32)
    pad_heads = lambda v: jnp.pad(v.reshape(1, -1).astype(F32), ((0, 0), (0, LANES - 2 * SSM_HEADS)))

    l = 0
    w = w_in[l]
    p = {
        "ln_in_g": row(ln_in_g), "ln_in_b": row(ln_in_b),
        "w_main": jnp.concatenate([w[:, :dt0], w[:, dt1:]], axis=1).astype(BF16),
        "w_dt": jnp.pad(w[:, dt0:dt1], ((0, 0), (0, LANES - 2 * SSM_HEADS))).astype(BF16),
        "conv_w": conv_w[l].astype(F32), "conv_b": row(conv_b[l]),
        "dtb_row": pad_heads(dt_bias[l]), "alog_row": pad_heads(a_log[l]),
        "dskip_row": row(jnp.repeat(d_skip[l], SSM_HEAD_DIM)),
        "r_f": _expansion_matrix(0), "r_b": _expansion_matrix(SSM_HEADS),
        "ssm_norm_g": row(ssm_norm_g[l]),
        "diff_lambda": diff_lambda[l].astype(F32), "subln_g": row(subln_g[l]),
        "lam_init": 0.8 - 0.6 * math.exp(-0.3 * l),
        "w_mem_kv": w_mem_kv[l].astype(BF16), "w_out": w_out[l].astype(BF16),
        "ln_g": row(ln_g[l]), "ln_b": row(ln_b[l]),
    }
    return (_trunk(x_prompt, mem_prompt, p), _trunk(x_sample, mem_sample, p))
```

```python
import functools
import math

import jax
import jax.numpy as jnp
from jax import lax
from jax.experimental import pallas as pl
from jax.experimental.pallas import tpu as pltpu

F32 = jnp.float32
BF16 = jnp.bfloat16

DEPTH = 1
EPS = 1e-5
ATT_HEADS = 8
ATT_HEAD_DIM = 128
ATT_V_DIM = 2 * ATT_HEAD_DIM
SSM_HEADS = 32
SSM_HEAD_DIM = 64
SSM_GROUPS = 4
SSM_STATE = 128
HEADS_PER_GROUP = SSM_HEADS // SSM_GROUPS
GROUP_WIDTH = HEADS_PER_GROUP * SSM_HEAD_DIM
D_CONV = 5
CHUNK = 128
MEM_HEADS = 4
MEM_HEAD_DIM = 128
LANES = 128
LOG2E = 1.4426950408889634
CONV_HALO_ROWS = 16
ALPHA = (2.0 * DEPTH) ** 0.25
VMEM_LIMIT_BYTES = 56 * 1024 * 1024


def _cparams(sem):
    return pltpu.CompilerParams(dimension_semantics=sem, vmem_limit_bytes=VMEM_LIMIT_BYTES)


def _layer_norm(x, g, b):
    mu = jnp.mean(x, axis=-1, keepdims=True)
    xc = x - mu
    var = jnp.mean(xc * xc, axis=-1, keepdims=True)
    return xc * lax.rsqrt(var + EPS) * g + b


def _silu(x):
    return x * jax.nn.sigmoid(x)


def _dot(a, b):
    return jnp.dot(a, b, preferred_element_type=F32)


def _dot_nt(a, b):
    return lax.dot_general(a, b, (((1,), (1,)), ((), ())), preferred_element_type=F32)


def _in_proj_kernel(x_ref, g_ref, b_ref, w_ref, wdt_ref, proj_ref, dt_ref, xn_ref):
    @pl.when(pl.program_id(1) == 0)
    def _():
        xn = _layer_norm(x_ref[...], g_ref[...], b_ref[...]).astype(BF16)
        xn_ref[...] = xn
        dt_ref[...] = _dot(xn, wdt_ref[...])

    proj_ref[...] = _dot(xn_ref[...], w_ref[...]).astype(BF16)


def _in_proj(x2, ln_g, ln_b, w_main, w_dt, *, tm, tn):
    m, d = x2.shape
    n = w_main.shape[1]
    return pl.pallas_call(
        _in_proj_kernel,
        out_shape=(jax.ShapeDtypeStruct((m, n), BF16), jax.ShapeDtypeStruct((m, LANES), F32)),
        grid=(m // tm, n // tn),
        in_specs=[
            pl.BlockSpec((tm, d), lambda i, j: (i, 0)),
            pl.BlockSpec((1, d), lambda i, j: (0, 0)),
            pl.BlockSpec((1, d), lambda i, j: (0, 0)),
            pl.BlockSpec((d, tn), lambda i, j: (0, j)),
            pl.BlockSpec((d, LANES), lambda i, j: (0, 0)),
        ],
        out_specs=(
            pl.BlockSpec((tm, tn), lambda i, j: (i, j)),
            pl.BlockSpec((tm, LANES), lambda i, j: (i, 0)),
        ),
        scratch_shapes=[pltpu.VMEM((tm, d), BF16)],
        compiler_params=_cparams(("parallel", "arbitrary")),
        name="in_proj",
    )(x2, ln_g, ln_b, w_main, w_dt)


def _attn_kernel(dl_ref, subg_ref, q_ref, k_ref, v_ref, g_ref, o_ref, acc_ref, qt_ref, vt_ref, st_ref,
                 *, qb, kb, lam_init):
    h = pl.program_id(1)
    i = pl.program_id(2)
    nblk = k_ref.shape[0] // kb
    ratio = qb // kb
    d = ATT_HEAD_DIM
    c = jnp.exp2(-(h + 1).astype(F32)) * LOG2E

    @pl.when(i == 0)
    def _():
        def tr(jj, carry):
            start = pl.multiple_of(jj * kb, kb)
            vt_ref[jj] = v_ref[pl.ds(start, kb), :].astype(F32).T.astype(BF16)
            return carry
        lax.fori_loop(0, nblk, tr, 0)

    def split3(x):
        hi = x.astype(BF16).astype(F32)
        mid = (x - hi).astype(BF16).astype(F32)
        return hi, mid, x - hi - mid

    lane = lax.broadcasted_iota(jnp.int32, (kb, d), 1)
    ku = split3(c * lax.broadcasted_iota(jnp.int32, (kb, d), 0).astype(F32))
    aug_k = jnp.where(lane < 3, -1.0, jnp.where(lane == 3, ku[0], jnp.where(lane == 4, ku[1],
                      jnp.where(lane == 5, ku[2], 0.0)))).astype(BF16)
    row = lax.broadcasted_iota(jnp.int32, (d, qb), 0)
    qr = split3(c * lax.broadcasted_iota(jnp.int32, (d, qb), 1).astype(F32))
    aug_qt = jnp.where(row == 0, qr[0], jnp.where(row == 1, qr[1], jnp.where(row == 2, qr[2],
                       jnp.where(row < 6, 1.0, 0.0))))
    q = q_ref[...].astype(F32) * (LOG2E / math.sqrt(d))
    for mp in range(2):
        qt = q[:, mp * d:(mp + 1) * d].T.astype(BF16)
        qt_ref[mp] = jnp.concatenate([qt, aug_qt.astype(BF16)], axis=0)
        qt_ref[2 + mp] = jnp.concatenate([qt, (-aug_qt).astype(BF16)], axis=0)

    acc_ref[...] = jnp.zeros_like(acc_ref)
    qw = st_ref.shape[1]
    nq = qb // qw
    chains = [(mp, hf) for mp in range(2) for hf in range(nq)]

    def key_operand(j, mp):
        start = pl.multiple_of(j * kb, kb)
        return jnp.concatenate([k_ref[pl.ds(start, kb), mp * d:(mp + 1) * d], aug_k], axis=1)

    def step(j, carry, variant, delta, overlap, next_variant):
        vt = vt_ref[j]
        kp = [key_operand(j, mp) for mp in range(2)]

        def scores(ch):
            mp, hf = ch
            return _dot(kp[mp], qt_ref[2 * variant + mp, :, hf * qw:(hf + 1) * qw])

        def finish(ch, st):
            mp, hf = ch
            if overlap is not None:
                ahead = (lax.broadcasted_iota(jnp.int32, (kb, qw), 0) - lax.broadcasted_iota(jnp.int32, (kb, qw), 1)
                         + (overlap - hf * qw))
                st = st + (-2.0 * c) * jnp.maximum(ahead, 0).astype(F32)
            m_old, l_old = carry[2 * mp][:, hf * qw:(hf + 1) * qw], carry[2 * mp + 1][:, hf * qw:(hf + 1) * qw]
            m_new = jnp.maximum(m_old, jnp.max(st, axis=0, keepdims=True) + delta)
            a = jnp.exp2(m_old - m_new)
            pt = jnp.exp2(st - (m_new - delta))
            l_new = a * l_old + jnp.sum(pt, axis=0, keepdims=True)
            acc_ref[mp, :, hf * qw:(hf + 1) * qw] = (a * acc_ref[mp, :, hf * qw:(hf + 1) * qw]
                                                     + _dot(vt, pt.astype(BF16)))
            return m_new, l_new

        stats = {}
        pending = st_ref[...]
        for n, ch in enumerate(chains):
            st = pending
            if n + 1 < len(chains):
                pending = scores(chains[n + 1])
            else:
                jn = jnp.minimum(j + 1, nblk - 1)
                st_ref[...] = _dot(key_operand(jn, 0), qt_ref[2 * next_variant, :, 0:qw])
            stats[ch] = finish(ch, st)
        new = []
        for mp in range(2):
            new.append(jnp.concatenate([stats[(mp, hf)][0] for hf in range(nq)], axis=1))
            new.append(jnp.concatenate([stats[(mp, hf)][1] for hf in range(nq)], axis=1))
        return tuple(new)

    def below(j, carry):
        return step(j, carry, 0, -c * (i * qb - j * kb).astype(F32), None, 0)

    def above(j, carry):
        return step(j, carry, 1, -c * (j * kb - i * qb).astype(F32), None, 1)

    neg = jnp.full((1, qb), -1e30, F32)
    zero = jnp.zeros((1, qb), F32)
    st_ref[...] = _dot(key_operand(0, 0), qt_ref[0, :, 0:qw])
    carry = lax.fori_loop(0, i * ratio, below, (neg, zero, neg, zero))
    for jj in range(ratio):
        carry = step(i * ratio + jj, carry, 0, c * float(jj * kb), jj * kb, 0 if jj + 1 < ratio else 1)
    _, l0, _, l1 = lax.fori_loop((i + 1) * ratio, nblk, above, carry)

    dl = dl_ref[...]
    lam = (jnp.exp(jnp.sum(dl[0:1] * dl[1:2], axis=-1, keepdims=True))
           - jnp.exp(jnp.sum(dl[2:3] * dl[3:4], axis=-1, keepdims=True)) + lam_init)
    out = (acc_ref[0] / l0 - lam * (acc_ref[1] / l1)).T
    ms = jnp.mean(out * out, axis=-1, keepdims=True)
    out = out * lax.rsqrt(ms + EPS) * subg_ref[...] * (1.0 - lam_init)
    o_ref[...] = (out * _silu(g_ref[...].astype(F32))).astype(BF16)


def _attention(proj3, diff_lambda, subln_g, *, lam_init, qb, kb):
    b, t, _ = proj3.shape
    hh = ATT_HEADS
    tq = qb
    kern = functools.partial(_attn_kernel, qb=qb, kb=kb, lam_init=lam_init)
    return pl.pallas_call(
        kern,
        out_shape=jax.ShapeDtypeStruct((b, t, hh * ATT_V_DIM), BF16),
        grid=(b, hh, t // tq),
        in_specs=[
            pl.BlockSpec((4, ATT_HEAD_DIM), lambda bi, h, i: (0, 0)),
            pl.BlockSpec((1, ATT_V_DIM), lambda bi, h, i: (0, 0)),
            pl.BlockSpec((None, tq, ATT_V_DIM), lambda bi, h, i: (bi, i, h)),
            pl.BlockSpec((None, t, ATT_V_DIM), lambda bi, h, i: (bi, 0, hh + h)),
            pl.BlockSpec((None, t, ATT_V_DIM), lambda bi, h, i: (bi, 0, 2 * hh + h)),
            pl.BlockSpec((None, tq, ATT_V_DIM), lambda bi, h, i: (bi, i, 3 * hh + h)),
        ],
        out_specs=pl.BlockSpec((None, tq, ATT_V_DIM), lambda bi, h, i: (bi, i, h)),
        scratch_shapes=[
            pltpu.VMEM((2, ATT_V_DIM, qb), F32),
            pltpu.VMEM((4, 2 * ATT_HEAD_DIM, qb), BF16),
            pltpu.VMEM((t // kb, ATT_V_DIM, kb), BF16),
            pltpu.VMEM((kb, min(qb, 2 * LANES)), F32),
        ],
        compiler_params=_cparams(("parallel", "parallel", "arbitrary")),
        name="diff_attn",
    )(diff_lambda, subln_g, proj3, proj3, proj3, proj3)


def _conv_kernel(cur_ref, prev_ref, next_ref, w_ref, b_ref, o_ref, u_ref, *, tt):
    i = pl.program_id(1)
    halo = CONV_HALO_ROWS
    pad = (D_CONV - 1) // 2
    prev = prev_ref[...].astype(F32)
    nxt = next_ref[...].astype(F32)
    u_ref[0:halo, :] = jnp.where(i > 0, prev, 0.0)
    u_ref[halo:halo + tt, :] = cur_ref[...].astype(F32)
    u_ref[halo + tt:2 * halo + tt, :] = jnp.where(i < pl.num_programs(1) - 1, nxt, 0.0)
    acc = u_ref[pl.ds(halo - pad, tt), :] * w_ref[0:1, :]
    for k in range(1, D_CONV):
        acc = acc + u_ref[pl.ds(halo - pad + k, tt), :] * w_ref[k:k + 1, :]
    o_ref[...] = _silu(acc + b_ref[...]).astype(BF16)


def _conv(proj3, conv_w, conv_b, *, col0, tt, tc):
    b, t, _ = proj3.shape
    width = conv_w.shape[1]
    halo = CONV_HALO_ROWS
    cb0 = col0 // tc
    rpb = tt // halo
    last = t // halo - 1
    kern = functools.partial(_conv_kernel, tt=tt)
    return pl.pallas_call(
        kern,
        out_shape=jax.ShapeDtypeStruct((b, t, width), BF16),
        grid=(b, t // tt, width // tc),
        in_specs=[
            pl.BlockSpec((None, tt, tc), lambda bi, i, c: (bi, i, cb0 + c)),
            pl.BlockSpec((None, halo, tc), lambda bi, i, c: (bi, jnp.maximum(i * rpb - 1, 0), cb0 + c)),
            pl.BlockSpec((None, halo, tc), lambda bi, i, c: (bi, jnp.minimum((i + 1) * rpb, last), cb0 + c)),
            pl.BlockSpec((D_CONV, tc), lambda bi, i, c: (0, c)),
            pl.BlockSpec((1, tc), lambda bi, i, c: (0, c)),
        ],
        out_specs=pl.BlockSpec((None, tt, tc), lambda bi, i, c: (bi, i, c)),
        scratch_shapes=[pltpu.VMEM((tt + 2 * halo, tc), F32)],
        compiler_params=_cparams(("parallel", "parallel", "parallel")),
        name="conv_silu",
    )(proj3, proj3, proj3, conv_w, conv_b)


def _cumsum_rows(x):
    row = lax.broadcasted_iota(jnp.int32, x.shape, 0)
    sh = 1
    while sh < x.shape[0]:
        x = x + jnp.where(row >= sh, pltpu.roll(x, sh, 0), 0.0)
        sh *= 2
    return x


def _ssd_direction(xa_ref, dtr_ref, dtb_ref, alog_ref, r_ref, s_ref, y_ref, dskip_ref, *, head0, backward):
    L = CHUNK
    d_ssm = SSM_HEADS * SSM_HEAD_DIM
    n_bc = SSM_GROUPS * SSM_STATE
    dt = jax.nn.softplus(dtr_ref[...] + dtb_ref[...])
    a = dt * (-jnp.exp(alog_ref[...]))
    p = _cumsum_rows(a)
    tot = p[L - 1:L, :]
    ac = (tot - p + a) if backward else p
    eo = jnp.exp(ac)
    ds = jnp.exp(tot - ac)
    cd = jnp.broadcast_to(jnp.exp(tot), (8, LANES))
    stack = jnp.concatenate([dt, eo, ds, cd], axis=0)
    hi = stack.astype(BF16)
    lo = (stack - hi.astype(F32)).astype(BF16)
    rr = r_ref[...]
    ex = _dot(hi, rr) + _dot(lo, rr)
    dtx, eox, dsx, cdx = ex[0:L], ex[L:2 * L], ex[2 * L:3 * L], ex[3 * L:3 * L + 1]

    xs = xa_ref[:, 0:d_ssm].astype(F32)
    xd = xs * dtx
    xdb = xd.astype(BF16)
    wst = (xd * dsx).astype(BF16)
    ac_t = ac.T
    row = lax.broadcasted_iota(jnp.int32, (L, L), 0)
    col = lax.broadcasted_iota(jnp.int32, (L, L), 1)
    mask = (row <= col) if backward else (row >= col)
    lane_lo = col < SSM_HEAD_DIM
    zero_b = jnp.zeros((L, L), BF16)

    for g in range(SSM_GROUPS):
        c0 = g * GROUP_WIDTH
        bg = xa_ref[:, d_ssm + g * SSM_STATE:d_ssm + (g + 1) * SSM_STATE]
        cg = xa_ref[:, d_ssm + n_bc + g * SSM_STATE:d_ssm + n_bc + (g + 1) * SSM_STATE]
        cb = _dot_nt(cg, bg)
        st = s_ref[g]
        y = _dot(cg, st.astype(BF16)) * eox[:, c0:c0 + GROUP_WIDTH]
        if dskip_ref is not None:
            y = y + xs[:, c0:c0 + GROUP_WIDTH] * dskip_ref[:, c0:c0 + GROUP_WIDTH]
        pieces = []
        for pp in range(HEADS_PER_GROUP // 2):
            ms = []
            for e in (2 * pp, 2 * pp + 1):
                hc = head0 + g * HEADS_PER_GROUP + e
                seg = ac[:, hc:hc + 1] - ac_t[hc:hc + 1, :]
                lm = jnp.exp(jnp.where(mask, seg, -jnp.inf))
                ms.append((cb * lm).astype(BF16))
            xpair = xdb[:, c0 + pp * LANES:c0 + (pp + 1) * LANES]
            rhs = jnp.concatenate([jnp.where(lane_lo, xpair, zero_b), jnp.where(lane_lo, zero_b, xpair)], axis=0)
            pieces.append(_dot(jnp.concatenate(ms, axis=1), rhs))
        y = y + jnp.concatenate(pieces, axis=1)
        y_ref[:, c0:c0 + GROUP_WIDTH] = y.astype(BF16)
        bt = bg.astype(F32).T.astype(BF16)
        s_ref[g] = st * cdx[:, c0:c0 + GROUP_WIDTH] + _dot(bt, wst[:, c0:c0 + GROUP_WIDTH])


def _ssd_kernel(dtb_ref, alog_ref, dskip_ref, rf_ref, rb_ref, xaf_ref, dtf_ref, xab_ref, dtbk_ref,
                yf_ref, yb_ref, sf_ref, sb_ref):
    @pl.when(pl.program_id(1) == 0)
    def _():
        sf_ref[...] = jnp.zeros_like(sf_ref)
        sb_ref[...] = jnp.zeros_like(sb_ref)

    _ssd_direction(xaf_ref, dtf_ref, dtb_ref, alog_ref, rf_ref, sf_ref, yf_ref, dskip_ref,
                   head0=0, backward=False)
    _ssd_direction(xab_ref, dtbk_ref, dtb_ref, alog_ref, rb_ref, sb_ref, yb_ref, None,
                   head0=SSM_HEADS, backward=True)


def _ssd(xbc_act, dt_raw3, dtb_row, alog_row, dskip_row, r_f, r_b):
    b, t, wa = xbc_act.shape
    nc = t // CHUNK
    d_ssm = SSM_HEADS * SSM_HEAD_DIM
    const = lambda bi, c: (0, 0)
    return pl.pallas_call(
        _ssd_kernel,
        out_shape=(jax.ShapeDtypeStruct((b, t, d_ssm), BF16), jax.ShapeDtypeStruct((b, t, d_ssm), BF16)),
        grid=(b, nc),
        in_specs=[
            pl.BlockSpec((1, LANES), const),
            pl.BlockSpec((1, LANES), const),
            pl.BlockSpec((1, d_ssm), const),
            pl.BlockSpec((LANES, d_ssm), const),
            pl.BlockSpec((LANES, d_ssm), const),
            pl.BlockSpec((None, CHUNK, wa), lambda bi, c: (bi, c, 0)),
            pl.BlockSpec((None, CHUNK, LANES), lambda bi, c: (bi, c, 0)),
            pl.BlockSpec((None, CHUNK, wa), lambda bi, c: (bi, nc - 1 - c, 0)),
            pl.BlockSpec((None, CHUNK, LANES), lambda bi, c: (bi, nc - 1 - c, 0)),
        ],
        out_specs=(
            pl.BlockSpec((None, CHUNK, d_ssm), lambda bi, c: (bi, c, 0)),
            pl.BlockSpec((None, CHUNK, d_ssm), lambda bi, c: (bi, nc - 1 - c, 0)),
        ),
        scratch_shapes=[
            pltpu.VMEM((SSM_GROUPS, SSM_STATE, GROUP_WIDTH), F32),
            pltpu.VMEM((SSM_GROUPS, SSM_STATE, GROUP_WIDTH), F32),
        ],
        compiler_params=_cparams(("parallel", "arbitrary")),
        name="ssd_scan",
    )(dtb_row, alog_row, dskip_row, r_f, r_b, xbc_act, dt_raw3, xbc_act, dt_raw3)


def _gate_kernel(yf_ref, yb_ref, z_ref, g_ref, o_ref):
    y = (yf_ref[...].astype(F32) + yb_ref[...].astype(F32)) * _silu(z_ref[...].astype(F32))
    for g in range(SSM_GROUPS):
        c0 = g * GROUP_WIDTH
        yg = y[:, c0:c0 + GROUP_WIDTH]
        ms = jnp.mean(yg * yg, axis=-1, keepdims=True)
        o_ref[:, c0:c0 + GROUP_WIDTH] = (yg * lax.rsqrt(ms + EPS) * g_ref[:, c0:c0 + GROUP_WIDTH]).astype(BF16)


def _gate(yf2, yb2, proj2, norm_g, *, z_col0, tm):
    m, d = yf2.shape
    zb = z_col0 // d
    return pl.pallas_call(
        _gate_kernel,
        out_shape=jax.ShapeDtypeStruct((m, d), BF16),
        grid=(m // tm,),
        in_specs=[
            pl.BlockSpec((tm, d), lambda i: (i, 0)),
            pl.BlockSpec((tm, d), lambda i: (i, 0)),
            pl.BlockSpec((tm, d), lambda i: (i, zb)),
            pl.BlockSpec((1, d), lambda i: (0, 0)),
        ],
        out_specs=pl.BlockSpec((tm, d), lambda i: (i, 0)),
        compiler_params=_cparams(("parallel",)),
        name="ssm_gate_norm",
    )(yf2, yb2, proj2, norm_g)


def _mem_kv_kernel(mem_ref, w_ref, o_ref):
    o_ref[...] = _dot(mem_ref[...].astype(BF16), w_ref[...]).astype(BF16)


def _mem_kv(mem, w_kv):
    b, mt, d = mem.shape
    n = w_kv.shape[1]
    return pl.pallas_call(
        _mem_kv_kernel,
        out_shape=jax.ShapeDtypeStruct((b, mt, n), BF16),
        grid=(b,),
        in_specs=[
            pl.BlockSpec((None, mt, d), lambda bi: (bi, 0, 0)),
            pl.BlockSpec((d, n), lambda bi: (0, 0)),
        ],
        out_specs=pl.BlockSpec((None, mt, n), lambda bi: (bi, 0, 0)),
        compiler_params=_cparams(("parallel",)),
        name="mem_kv_proj",
    )(mem, w_kv)


def _mem_attn_kernel(q_ref, g_ref, kv_ref, o_ref):
    d = MEM_HEAD_DIM
    d_mem = MEM_HEADS * d
    scale = 1.0 / math.sqrt(d)
    for hh in range(MEM_HEADS):
        q = q_ref[:, hh * d:(hh + 1) * d]
        k = kv_ref[:, hh * d:(hh + 1) * d]
        v = kv_ref[:, d_mem + hh * d:d_mem + (hh + 1) * d]
        s = _dot_nt(q, k) * scale
        p = jnp.exp(s - jnp.max(s, axis=-1, keepdims=True))
        o = _dot(p.astype(BF16), v) / jnp.sum(p, axis=-1, keepdims=True)
        o_ref[:, hh * d:(hh + 1) * d] = (o * _silu(g_ref[:, hh * d:(hh + 1) * d].astype(F32))).astype(BF16)


def _mem_attn(proj3, kv, *, q_col0, tq):
    b, t, _ = proj3.shape
    mt, n = kv.shape[1:]
    d_mem = MEM_HEADS * MEM_HEAD_DIM
    qb = q_col0 // d_mem
    return pl.pallas_call(
        _mem_attn_kernel,
        out_shape=jax.ShapeDtypeStruct((b, t, d_mem), BF16),
        grid=(b, t // tq),
        in_specs=[
            pl.BlockSpec((None, tq, d_mem), lambda bi, i: (bi, i, qb)),
            pl.BlockSpec((None, tq, d_mem), lambda bi, i: (bi, i, qb + 1)),
            pl.BlockSpec((None, mt, n), lambda bi, i: (bi, 0, 0)),
        ],
        out_specs=pl.BlockSpec((None, tq, d_mem), lambda bi, i: (bi, i, 0)),
        compiler_params=_cparams(("parallel", "parallel")),
        name="mem_attn",
    )(proj3, proj3, kv)


def _out_proj_kernel(x_ref, gi_ref, bi_ref, ha_ref, hs_ref, hm_ref, w_ref, g_ref, b_ref, o_ref):
    da = ha_ref.shape[1]
    dsm = hs_ref.shape[1]
    xn = _layer_norm(x_ref[...], gi_ref[...], bi_ref[...])
    out = _dot(ha_ref[...], w_ref[0:da, :])
    out = out + _dot(hs_ref[...], w_ref[da:da + dsm, :])
    out = out + _dot(hm_ref[...], w_ref[da + dsm:, :])
    o_ref[...] = _layer_norm(ALPHA * xn + out, g_ref[...], b_ref[...])


def _out_proj(x2, ln_in_g, ln_in_b, h_att, h_ssm, h_mem, w_out, ln_g, ln_b, *, tm):
    m, d = x2.shape
    da, dsm, dm = h_att.shape[1], h_ssm.shape[1], h_mem.shape[1]
    row = lambda i: (i, 0)
    const = lambda i: (0, 0)
    return pl.pallas_call(
        _out_proj_kernel,
        out_shape=jax.ShapeDtypeStruct((m, d), F32),
        grid=(m // tm,),
        in_specs=[
            pl.BlockSpec((tm, d), row),
            pl.BlockSpec((1, d), const),
            pl.BlockSpec((1, d), const),
            pl.BlockSpec((tm, da), row),
            pl.BlockSpec((tm, dsm), row),
            pl.BlockSpec((tm, dm), row),
            pl.BlockSpec(w_out.shape, const, pipeline_mode=pl.Buffered(1)),
            pl.BlockSpec((1, d), const),
            pl.BlockSpec((1, d), const),
        ],
        out_specs=pl.BlockSpec((tm, d), row),
        compiler_params=_cparams(("parallel",)),
        name="out_proj_ln",
    )(x2, ln_in_g, ln_in_b, h_att, h_ssm, h_mem, w_out, ln_g, ln_b)


def _tile(n, pref):
    t = min(n, pref)
    while n % t:
        t //= 2
    return t


def _trunk(x, mem, p):
    b, t, d = x.shape
    m = b * t
    x2 = x.reshape(m, d)
    proj2, dt_raw = _in_proj(x2, p["ln_in_g"], p["ln_in_b"], p["w_main"], p["w_dt"],
                             tm=_tile(m, 1024), tn=512)
    n = proj2.shape[1]
    proj3 = proj2.reshape(b, t, n)
    d_att = ATT_HEADS * ATT_V_DIM
    d_ssm = SSM_HEADS * SSM_HEAD_DIM

    h_att = _attention(proj3, p["diff_lambda"], p["subln_g"], lam_init=p["lam_init"],
                       qb=_tile(t, 2048), kb=_tile(t, 512))

    xbc_act = _conv(proj3, p["conv_w"], p["conv_b"], col0=4 * d_att + d_ssm, tt=_tile(t, 512), tc=1024)
    y_f, y_b = _ssd(xbc_act, dt_raw.reshape(b, t, LANES), p["dtb_row"], p["alog_row"], p["dskip_row"],
                    p["r_f"], p["r_b"])
    h_ssm = _gate(y_f.reshape(m, d_ssm), y_b.reshape(m, d_ssm), proj2, p["ssm_norm_g"],
                  z_col0=4 * d_att, tm=_tile(m, 512))

    kv = _mem_kv(mem, p["w_mem_kv"])
    h_mem = _mem_attn(proj3, kv, q_col0=4 * d_att + d_ssm + p["conv_w"].shape[1], tq=_tile(t, 512))

    y = _out_proj(x2, p["ln_in_g"], p["ln_in_b"], h_att.reshape(m, d_att), h_ssm,
                  h_mem.reshape(m, MEM_HEADS * MEM_HEAD_DIM), p["w_out"], p["ln_g"], p["ln_b"],
                  tm=_tile(m, 512))
    return y.reshape(b, t, d)


def _expansion_matrix(head0):
    rows = lax.broadcasted_iota(jnp.int32, (LANES, SSM_HEADS * SSM_HEAD_DIM), 0)
    cols = lax.broadcasted_iota(jnp.int32, (LANES, SSM_HEADS * SSM_HEAD_DIM), 1)
    return (rows == head0 + cols // SSM_HEAD_DIM).astype(BF16)


def kernel(x_prompt, x_sample, mem_prompt, mem_sample, ln_in_g, ln_in_b, w_in, conv_w, conv_b, dt_bias, a_log, d_skip, ssm_norm_g, diff_lambda, subln_g, w_mem_kv, w_out, ln_g, ln_b):
    assert w_in.shape[0] == DEPTH
    d_att = ATT_HEADS * ATT_V_DIM
    d_ssm = SSM_HEADS * SSM_HEAD_DIM
    d_xbc = conv_w.shape[-1]
    dt0 = 4 * d_att + d_ssm + d_xbc
    dt1 = dt0 + 2 * SSM_HEADS
    row = lambda v: v.reshape(1, -1).astype(F32)
    pad_heads = lambda v: jnp.pad(v.reshape(1, -1).astype(F32), ((0, 0), (0, LANES - 2 * SSM_HEADS)))

    l = 0
    w = w_in[l]
    p = {
        "ln_in_g": row(ln_in_g), "ln_in_b": row(ln_in_b),
        "w_main": jnp.concatenate([w[:, :dt0], w[:, dt1:]], axis=1).astype(BF16),
        "w_dt": jnp.pad(w[:, dt0:dt1], ((0, 0), (0, LANES - 2 * SSM_HEADS))).astype(BF16),
        "conv_w": conv_w[l].astype(F32), "conv_b": row(conv_b[l]),
        "dtb_row": pad_heads(dt_bias[l]), "alog_row": pad_heads(a_log[l]),
        "dskip_row": row(jnp.repeat(d_skip[l], SSM_HEAD_DIM)),
        "r_f": _expansion_matrix(0), "r_b": _expansion_matrix(SSM_HEADS),
        "ssm_norm_g": row(ssm_norm_g[l]),
        "diff_lambda": diff_lambda[l].astype(F32), "subln_g": row(subln_g[l]),
        "lam_init": 0.8 - 0.6 * math.exp(-0.3 * l),
        "w_mem_kv": w_mem_kv[l].astype(BF16), "w_out": w_out[l].astype(BF16),
        "ln_g": row(ln_g[l]), "ln_b": row(ln_b[l]),
    }
    return (_trunk(x_prompt, mem_prompt, p), _trunk(x_sample, mem_sample, p))
```

```python
import functools
import math

import jax
import jax.numpy as jnp
from jax import lax
from jax.experimental import pallas as pl
from jax.experimental.pallas import tpu as pltpu

F32 = jnp.float32
BF16 = jnp.bfloat16

DEPTH = 1
EPS = 1e-5
ATT_HEADS = 8
ATT_HEAD_DIM = 128
ATT_V_DIM = 2 * ATT_HEAD_DIM
SSM_HEADS = 32
SSM_HEAD_DIM = 64
SSM_GROUPS = 4
SSM_STATE = 128
HEADS_PER_GROUP = SSM_HEADS // SSM_GROUPS
GROUP_WIDTH = HEADS_PER_GROUP * SSM_HEAD_DIM
D_CONV = 5
CHUNK = 128
MEM_HEADS = 4
MEM_HEAD_DIM = 128
LANES = 128
LOG2E = 1.4426950408889634
CONV_HALO_ROWS = 16
ALPHA = (2.0 * DEPTH) ** 0.25
VMEM_LIMIT_BYTES = 56 * 1024 * 1024


def _cparams(sem):
    return pltpu.CompilerParams(dimension_semantics=sem, vmem_limit_bytes=VMEM_LIMIT_BYTES)


def _layer_norm(x, g, b):
    mu = jnp.mean(x, axis=-1, keepdims=True)
    xc = x - mu
    var = jnp.mean(xc * xc, axis=-1, keepdims=True)
    return xc * lax.rsqrt(var + EPS) * g + b


def _silu(x):
    return x * jax.nn.sigmoid(x)


def _dot(a, b):
    return jnp.dot(a, b, preferred_element_type=F32)


def _dot_nt(a, b):
    return lax.dot_general(a, b, (((1,), (1,)), ((), ())), preferred_element_type=F32)


def _in_proj_kernel(x_ref, g_ref, b_ref, w_ref, wdt_ref, proj_ref, dt_ref, xn_ref):
    @pl.when(pl.program_id(1) == 0)
    def _():
        xn = _layer_norm(x_ref[...], g_ref[...], b_ref[...]).astype(BF16)
        xn_ref[...] = xn
        dt_ref[...] = _dot(xn, wdt_ref[...])

    proj_ref[...] = _dot(xn_ref[...], w_ref[...]).astype(BF16)


def _in_proj(x2, ln_g, ln_b, w_main, w_dt, *, tm, tn):
    m, d = x2.shape
    n = w_main.shape[1]
    return pl.pallas_call(
        _in_proj_kernel,
        out_shape=(jax.ShapeDtypeStruct((m, n), BF16), jax.ShapeDtypeStruct((m, LANES), F32)),
        grid=(m // tm, n // tn),
        in_specs=[
            pl.BlockSpec((tm, d), lambda i, j: (i, 0)),
            pl.BlockSpec((1, d), lambda i, j: (0, 0)),
            pl.BlockSpec((1, d), lambda i, j: (0, 0)),
            pl.BlockSpec((d, tn), lambda i, j: (0, j)),
            pl.BlockSpec((d, LANES), lambda i, j: (0, 0)),
        ],
        out_specs=(
            pl.BlockSpec((tm, tn), lambda i, j: (i, j)),
            pl.BlockSpec((tm, LANES), lambda i, j: (i, 0)),
        ),
        scratch_shapes=[pltpu.VMEM((tm, d), BF16)],
        compiler_params=_cparams(("parallel", "arbitrary")),
        name="in_proj",
    )(x2, ln_g, ln_b, w_main, w_dt)


def _attn_kernel(dl_ref, subg_ref, q_ref, k_ref, v_ref, g_ref, o_ref, acc_ref, qt_ref, vt_ref, st_ref,
                 *, qb, kb, lam_init):
    h = pl.program_id(1)
    i = pl.program_id(2)
    nblk = k_ref.shape[0] // kb
    ratio = qb // kb
    d = ATT_HEAD_DIM
    c = jnp.exp2(-(h + 1).astype(F32)) * LOG2E

    @pl.when(i == 0)
    def _():
        def tr(jj, carry):
            start = pl.multiple_of(jj * kb, kb)
            vt_ref[jj] = v_ref[pl.ds(start, kb), :].astype(F32).T.astype(BF16)
            return carry
        lax.fori_loop(0, nblk, tr, 0)

    def split3(x):
        hi = x.astype(BF16).astype(F32)
        mid = (x - hi).astype(BF16).astype(F32)
        return hi, mid, x - hi - mid

    lane = lax.broadcasted_iota(jnp.int32, (kb, d), 1)
    ku = split3(c * lax.broadcasted_iota(jnp.int32, (kb, d), 0).astype(F32))
    aug_k = jnp.where(lane < 3, -1.0, jnp.where(lane == 3, ku[0], jnp.where(lane == 4, ku[1],
                      jnp.where(lane == 5, ku[2], 0.0)))).astype(BF16)
    row = lax.broadcasted_iota(jnp.int32, (d, qb), 0)
    qr = split3(c * lax.broadcasted_iota(jnp.int32, (d, qb), 1).astype(F32))
    aug_qt = jnp.where(row == 0, qr[0], jnp.where(row == 1, qr[1], jnp.where(row == 2, qr[2],
                       jnp.where(row < 6, 1.0, 0.0))))
    q = q_ref[...].astype(F32) * (LOG2E / math.sqrt(d))
    for mp in range(2):
        qt = q[:, mp * d:(mp + 1) * d].T.astype(BF16)
        qt_ref[mp] = jnp.concatenate([qt, aug_qt.astype(BF16)], axis=0)
        qt_ref[2 + mp] = jnp.concatenate([qt, (-aug_qt).astype(BF16)], axis=0)

    acc_ref[...] = jnp.zeros_like(acc_ref)
    qw = st_ref.shape[1]
    nq = qb // qw
    chains = [(mp, hf) for mp in range(2) for hf in range(nq)]

    def key_operand(j, mp):
        start = pl.multiple_of(j * kb, kb)
        return jnp.concatenate([k_ref[pl.ds(start, kb), mp * d:(mp + 1) * d], aug_k], axis=1)

    def step(j, carry, variant, delta, overlap, next_variant):
        vt = vt_ref[j]
        kp = [key_operand(j, mp) for mp in range(2)]

        def scores(ch):
            mp, hf = ch
            return _dot(kp[mp], qt_ref[2 * variant + mp, :, hf * qw:(hf + 1) * qw])

        def finish(ch, st):
            mp, hf = ch
            if overlap is not None:
                ahead = (lax.broadcasted_iota(jnp.int32, (kb, qw), 0) - lax.broadcasted_iota(jnp.int32, (kb, qw), 1)
                         + (overlap - hf * qw))
                st = st + (-2.0 * c) * jnp.maximum(ahead, 0).astype(F32)
            m_old, l_old = carry[2 * mp][:, hf * qw:(hf + 1) * qw], carry[2 * mp + 1][:, hf * qw:(hf + 1) * qw]
            m_new = jnp.maximum(m_old, jnp.max(st, axis=0, keepdims=True) + delta)
            a = jnp.exp2(m_old - m_new)
            pt = jnp.exp2(st - (m_new - delta))
            l_new = a * l_old + jnp.sum(pt, axis=0, keepdims=True)
            acc_ref[mp, :, hf * qw:(hf + 1) * qw] = (a * acc_ref[mp, :, hf * qw:(hf + 1) * qw]
                                                     + _dot(vt, pt.astype(BF16)))
            return m_new, l_new

        stats = {}
        pending = st_ref[...]
        for n, ch in enumerate(chains):
            st = pending
            if n + 1 < len(chains):
                pending = scores(chains[n + 1])
            else:
                jn = jnp.minimum(j + 1, nblk - 1)
                st_ref[...] = _dot(key_operand(jn, 0), qt_ref[2 * next_variant, :, 0:qw])
            stats[ch] = finish(ch, st)
        new = []
        for mp in range(2):
            new.append(jnp.concatenate([stats[(mp, hf)][0] for hf in range(nq)], axis=1))
            new.append(jnp.concatenate([stats[(mp, hf)][1] for hf in range(nq)], axis=1))
        return tuple(new)

    def below(j, carry):
        return step(j, carry, 0, -c * (i * qb - j * kb).astype(F32), None, 0)

    def above(j, carry):
        return step(j, carry, 1, -c * (j * kb - i * qb).astype(F32), None, 1)

    neg = jnp.full((1, qb), -1e30, F32)
    zero = jnp.zeros((1, qb), F32)
    st_ref[...] = _dot(key_operand(0, 0), qt_ref[0, :, 0:qw])
    carry = lax.fori_loop(0, i * ratio, below, (neg, zero, neg, zero))
    for jj in range(ratio):
        carry = step(i * ratio + jj, carry, 0, c * float(jj * kb), jj * kb, 0 if jj + 1 < ratio else 1)
    _, l0, _, l1 = lax.fori_loop((i + 1) * ratio, nblk, above, carry)

    dl = dl_ref[...]
    lam = (jnp.exp(jnp.sum(dl[0:1] * dl[1:2], axis=-1, keepdims=True))
           - jnp.exp(jnp.sum(dl[2:3] * dl[3:4], axis=-1, keepdims=True)) + lam_init)
    out = (acc_ref[0] / l0 - lam * (acc_ref[1] / l1)).T
    ms = jnp.mean(out * out, axis=-1, keepdims=True)
    out = out * lax.rsqrt(ms + EPS) * subg_ref[...] * (1.0 - lam_init)
    o_ref[...] = (out * _silu(g_ref[...].astype(F32))).astype(BF16)


def _attention(proj3, diff_lambda, subln_g, *, lam_init, qb, kb):
    b, t, _ = proj3.shape
    hh = ATT_HEADS
    tq = qb
    kern = functools.partial(_attn_kernel, qb=qb, kb=kb, lam_init=lam_init)
    return pl.pallas_call(
        kern,
        out_shape=jax.ShapeDtypeStruct((b, t, hh * ATT_V_DIM), BF16),
        grid=(b, hh, t // tq),
        in_specs=[
            pl.BlockSpec((4, ATT_HEAD_DIM), lambda bi, h, i: (0, 0)),
            pl.BlockSpec((1, ATT_V_DIM), lambda bi, h, i: (0, 0)),
            pl.BlockSpec((None, tq, ATT_V_DIM), lambda bi, h, i: (bi, i, h)),
            pl.BlockSpec((None, t, ATT_V_DIM), lambda bi, h, i: (bi, 0, hh + h)),
            pl.BlockSpec((None, t, ATT_V_DIM), lambda bi, h, i: (bi, 0, 2 * hh + h)),
            pl.BlockSpec((None, tq, ATT_V_DIM), lambda bi, h, i: (bi, i, 3 * hh + h)),
        ],
        out_specs=pl.BlockSpec((None, tq, ATT_V_DIM), lambda bi, h, i: (bi, i, h)),
        scratch_shapes=[
            pltpu.VMEM((2, ATT_V_DIM, qb), F32),
            pltpu.VMEM((4, 2 * ATT_HEAD_DIM, qb), BF16),
            pltpu.VMEM((t // kb, ATT_V_DIM, kb), BF16),
            pltpu.VMEM((kb, min(qb, 2 * LANES)), F32),
        ],
        compiler_params=_cparams(("parallel", "parallel", "arbitrary")),
        name="diff_attn",
    )(diff_lambda, subln_g, proj3, proj3, proj3, proj3)


def _conv_kernel(cur_ref, prev_ref, next_ref, w_ref, b_ref, o_ref, u_ref, *, tt):
    i = pl.program_id(1)
    halo = CONV_HALO_ROWS
    pad = (D_CONV - 1) // 2
    prev = prev_ref[...].astype(F32)
    nxt = next_ref[...].astype(F32)
    u_ref[0:halo, :] = jnp.where(i > 0, prev, 0.0)
    u_ref[halo:halo + tt, :] = cur_ref[...].astype(F32)
    u_ref[halo + tt:2 * halo + tt, :] = jnp.where(i < pl.num_programs(1) - 1, nxt, 0.0)
    acc = u_ref[pl.ds(halo - pad, tt), :] * w_ref[0:1, :]
    for k in range(1, D_CONV):
        acc = acc + u_ref[pl.ds(halo - pad + k, tt), :] * w_ref[k:k + 1, :]
    o_ref[...] = _silu(acc + b_ref[...]).astype(BF16)


def _conv(proj3, conv_w, conv_b, *, col0, tt, tc):
    b, t, _ = proj3.shape
    width = conv_w.shape[1]
    halo = CONV_HALO_ROWS
    cb0 = col0 // tc
    rpb = tt // halo
    last = t // halo - 1
    kern = functools.partial(_conv_kernel, tt=tt)
    return pl.pallas_call(
        kern,
        out_shape=jax.ShapeDtypeStruct((b, t, width), BF16),
        grid=(b, t // tt, width // tc),
        in_specs=[
            pl.BlockSpec((None, tt, tc), lambda bi, i, c: (bi, i, cb0 + c)),
            pl.BlockSpec((None, halo, tc), lambda bi, i, c: (bi, jnp.maximum(i * rpb - 1, 0), cb0 + c)),
            pl.BlockSpec((None, halo, tc), lambda bi, i, c: (bi, jnp.minimum((i + 1) * rpb, last), cb0 + c)),
            pl.BlockSpec((D_CONV, tc), lambda bi, i, c: (0, c)),
            pl.BlockSpec((1, tc), lambda bi, i, c: (0, c)),
        ],
        out_specs=pl.BlockSpec((None, tt, tc), lambda bi, i, c: (bi, i, c)),
        scratch_shapes=[pltpu.VMEM((tt + 2 * halo, tc), F32)],
        compiler_params=_cparams(("parallel", "parallel", "parallel")),
        name="conv_silu",
    )(proj3, proj3, proj3, conv_w, conv_b)


def _cumsum_rows(x):
    row = lax.broadcasted_iota(jnp.int32, x.shape, 0)
    sh = 1
    while sh < x.shape[0]:
        x = x + jnp.where(row >= sh, pltpu.roll(x, sh, 0), 0.0)
        sh *= 2
    return x


def _ssd_direction(xa_ref, dtr_ref, dtb_ref, alog_ref, r_ref, s_ref, y_ref, dskip_ref, *, head0, backward):
    L = CHUNK
    d_ssm = SSM_HEADS * SSM_HEAD_DIM
    n_bc = SSM_GROUPS * SSM_STATE
    dt = jax.nn.softplus(dtr_ref[...] + dtb_ref[...])
    a = dt * (-jnp.exp(alog_ref[...]))
    p = _cumsum_rows(a)
    tot = p[L - 1:L, :]
    ac = (tot - p + a) if backward else p
    eo = jnp.exp(ac)
    ds = jnp.exp(tot - ac)
    cd = jnp.broadcast_to(jnp.exp(tot), (8, LANES))
    stack = jnp.concatenate([dt, eo, ds, cd], axis=0)
    ex = _dot(stack.astype(BF16), r_ref[...])
    dtx, eox, dsx, cdx = ex[0:L], ex[L:2 * L], ex[2 * L:3 * L], ex[3 * L:3 * L + 1]

    xs = xa_ref[:, 0:d_ssm].astype(F32)
    xd = xs * dtx
    xdb = xd.astype(BF16)
    wst = (xd * dsx).astype(BF16)
    ac_t = ac.T
    row = lax.broadcasted_iota(jnp.int32, (L, L), 0)
    col = lax.broadcasted_iota(jnp.int32, (L, L), 1)
    mask = (row <= col) if backward else (row >= col)
    lane_lo = col < SSM_HEAD_DIM
    zero_b = jnp.zeros((L, L), BF16)

    for g in range(SSM_GROUPS):
        c0 = g * GROUP_WIDTH
        bg = xa_ref[:, d_ssm + g * SSM_STATE:d_ssm + (g + 1) * SSM_STATE]
        cg = xa_ref[:, d_ssm + n_bc + g * SSM_STATE:d_ssm + n_bc + (g + 1) * SSM_STATE]
        cb = _dot_nt(cg, bg)
        st = s_ref[g]
        y = _dot(cg, st.astype(BF16)) * eox[:, c0:c0 + GROUP_WIDTH]
        if dskip_ref is not None:
            y = y + xs[:, c0:c0 + GROUP_WIDTH] * dskip_ref[:, c0:c0 + GROUP_WIDTH]
        pieces = []
        for pp in range(HEADS_PER_GROUP // 2):
            ms = []
            for e in (2 * pp, 2 * pp + 1):
                hc = head0 + g * HEADS_PER_GROUP + e
                seg = ac[:, hc:hc + 1] - ac_t[hc:hc + 1, :]
                lm = jnp.exp(jnp.where(mask, seg, -jnp.inf))
                ms.append((cb * lm).astype(BF16))
            xpair = xdb[:, c0 + pp * LANES:c0 + (pp + 1) * LANES]
            rhs = jnp.concatenate([jnp.where(lane_lo, xpair, zero_b), jnp.where(lane_lo, zero_b, xpair)], axis=0)
            pieces.append(_dot(jnp.concatenate(ms, axis=1), rhs))
        y = y + jnp.concatenate(pieces, axis=1)
        y_ref[:, c0:c0 + GROUP_WIDTH] = y.astype(BF16)
        bt = bg.astype(F32).T.astype(BF16)
        s_ref[g] = st * cdx[:, c0:c0 + GROUP_WIDTH] + _dot(bt, wst[:, c0:c0 + GROUP_WIDTH])


def _ssd_kernel(dtb_ref, alog_ref, dskip_ref, rf_ref, rb_ref, xaf_ref, dtf_ref, xab_ref, dtbk_ref,
                yf_ref, yb_ref, sf_ref, sb_ref):
    @pl.when(pl.program_id(1) == 0)
    def _():
        sf_ref[...] = jnp.zeros_like(sf_ref)
        sb_ref[...] = jnp.zeros_like(sb_ref)

    _ssd_direction(xaf_ref, dtf_ref, dtb_ref, alog_ref, rf_ref, sf_ref, yf_ref, dskip_ref,
                   head0=0, backward=False)
    _ssd_direction(xab_ref, dtbk_ref, dtb_ref, alog_ref, rb_ref, sb_ref, yb_ref, None,
                   head0=SSM_HEADS, backward=True)


def _ssd(xbc_act, dt_raw3, dtb_row, alog_row, dskip_row, r_f, r_b):
    b, t, wa = xbc_act.shape
    nc = t // CHUNK
    d_ssm = SSM_HEADS * SSM_HEAD_DIM
    const = lambda bi, c: (0, 0)
    return pl.pallas_call(
        _ssd_kernel,
        out_shape=(jax.ShapeDtypeStruct((b, t, d_ssm), BF16), jax.ShapeDtypeStruct((b, t, d_ssm), BF16)),
        grid=(b, nc),
        in_specs=[
            pl.BlockSpec((1, LANES), const),
            pl.BlockSpec((1, LANES), const),
            pl.BlockSpec((1, d_ssm), const),
            pl.BlockSpec((LANES, d_ssm), const),
            pl.BlockSpec((LANES, d_ssm), const),
            pl.BlockSpec((None, CHUNK, wa), lambda bi, c: (bi, c, 0)),
            pl.BlockSpec((None, CHUNK, LANES), lambda bi, c: (bi, c, 0)),
            pl.BlockSpec((None, CHUNK, wa), lambda bi, c: (bi, nc - 1 - c, 0)),
            pl.BlockSpec((None, CHUNK, LANES), lambda bi, c: (bi, nc - 1 - c, 0)),
        ],
        out_specs=(
            pl.BlockSpec((None, CHUNK, d_ssm), lambda bi, c: (bi, c, 0)),
            pl.BlockSpec((None, CHUNK, d_ssm), lambda bi, c: (bi, nc - 1 - c, 0)),
        ),
        scratch_shapes=[
            pltpu.VMEM((SSM_GROUPS, SSM_STATE, GROUP_WIDTH), F32),
            pltpu.VMEM((SSM_GROUPS, SSM_STATE, GROUP_WIDTH), F32),
        ],
        compiler_params=_cparams(("parallel", "arbitrary")),
        name="ssd_scan",
    )(dtb_row, alog_row, dskip_row, r_f, r_b, xbc_act, dt_raw3, xbc_act, dt_raw3)


def _mem_kv_kernel(mem_ref, w_ref, o_ref):
    o_ref[...] = _dot(mem_ref[...].astype(BF16), w_ref[...]).astype(BF16)


def _mem_kv(mem, w_kv):
    b, mt, d = mem.shape
    n = w_kv.shape[1]
    return pl.pallas_call(
        _mem_kv_kernel,
        out_shape=jax.ShapeDtypeStruct((b, mt, n), BF16),
        grid=(b,),
        in_specs=[
            pl.BlockSpec((None, mt, d), lambda bi: (bi, 0, 0)),
            pl.BlockSpec((d, n), lambda bi: (0, 0)),
        ],
        out_specs=pl.BlockSpec((None, mt, n), lambda bi: (bi, 0, 0)),
        compiler_params=_cparams(("parallel",)),
        name="mem_kv_proj",
    )(mem, w_kv)


def _mem_attn_kernel(q_ref, g_ref, kv_ref, o_ref):
    d = MEM_HEAD_DIM
    d_mem = MEM_HEADS * d
    scale = 1.0 / math.sqrt(d)
    for hh in range(MEM_HEADS):
        q = q_ref[:, hh * d:(hh + 1) * d]
        k = kv_ref[:, hh * d:(hh + 1) * d]
        v = kv_ref[:, d_mem + hh * d:d_mem + (hh + 1) * d]
        s = _dot_nt(q, k) * scale
        p = jnp.exp(s - jnp.max(s, axis=-1, keepdims=True))
        o = _dot(p.astype(BF16), v) / jnp.sum(p, axis=-1, keepdims=True)
        o_ref[:, hh * d:(hh + 1) * d] = (o * _silu(g_ref[:, hh * d:(hh + 1) * d].astype(F32))).astype(BF16)


def _mem_attn(proj3, kv, *, q_col0, tq):
    b, t, _ = proj3.shape
    mt, n = kv.shape[1:]
    d_mem = MEM_HEADS * MEM_HEAD_DIM
    qb = q_col0 // d_mem
    return pl.pallas_call(
        _mem_attn_kernel,
        out_shape=jax.ShapeDtypeStruct((b, t, d_mem), BF16),
        grid=(b, t // tq),
        in_specs=[
            pl.BlockSpec((None, tq, d_mem), lambda bi, i: (bi, i, qb)),
            pl.BlockSpec((None, tq, d_mem), lambda bi, i: (bi, i, qb + 1)),
            pl.BlockSpec((None, mt, n), lambda bi, i: (bi, 0, 0)),
        ],
        out_specs=pl.BlockSpec((None, tq, d_mem), lambda bi, i: (bi, i, 0)),
        compiler_params=_cparams(("parallel", "parallel")),
        name="mem_attn",
    )(proj3, proj3, kv)


def _out_proj_kernel(x_ref, gi_ref, bi_ref, ha_ref, yf_ref, yb_ref, z_ref, ng_ref, hm_ref, w_ref, g_ref, b_ref,
                     o_ref):
    da = ha_ref.shape[1]
    dsm = yf_ref.shape[1]
    out = _dot(ha_ref[...], w_ref[0:da, :])
    out = out + _dot(hm_ref[...], w_ref[da + dsm:, :])
    for g in range(SSM_GROUPS):
        c0 = g * GROUP_WIDTH
        yg = ((yf_ref[:, c0:c0 + GROUP_WIDTH].astype(F32) + yb_ref[:, c0:c0 + GROUP_WIDTH].astype(F32))
              * _silu(z_ref[:, c0:c0 + GROUP_WIDTH].astype(F32)))
        ms = jnp.mean(yg * yg, axis=-1, keepdims=True)
        hg = (yg * lax.rsqrt(ms + EPS) * ng_ref[:, c0:c0 + GROUP_WIDTH]).astype(BF16)
        out = out + _dot(hg, w_ref[da + c0:da + c0 + GROUP_WIDTH, :])
    xn = _layer_norm(x_ref[...], gi_ref[...], bi_ref[...])
    o_ref[...] = _layer_norm(ALPHA * xn + out, g_ref[...], b_ref[...])


def _out_proj(x2, ln_in_g, ln_in_b, h_att, y_f, y_b, proj2, norm_g, h_mem, w_out, ln_g, ln_b, *, z_col0, tm):
    m, d = x2.shape
    da, dsm, dm = h_att.shape[1], y_f.shape[1], h_mem.shape[1]
    zb = z_col0 // dsm
    row = lambda i: (i, 0)
    const = lambda i: (0, 0)
    return pl.pallas_call(
        _out_proj_kernel,
        out_shape=jax.ShapeDtypeStruct((m, d), F32),
        grid=(m // tm,),
        in_specs=[
            pl.BlockSpec((tm, d), row),
            pl.BlockSpec((1, d), const),
            pl.BlockSpec((1, d), const),
            pl.BlockSpec((tm, da), row),
            pl.BlockSpec((tm, dsm), row),
            pl.BlockSpec((tm, dsm), row),
            pl.BlockSpec((tm, dsm), lambda i: (i, zb)),
            pl.BlockSpec((1, dsm), const),
            pl.BlockSpec((tm, dm), row),
            pl.BlockSpec(w_out.shape, const, pipeline_mode=pl.Buffered(1)),
            pl.BlockSpec((1, d), const),
            pl.BlockSpec((1, d), const),
        ],
        out_specs=pl.BlockSpec((tm, d), row),
        compiler_params=_cparams(("parallel",)),
        name="out_proj_ln",
    )(x2, ln_in_g, ln_in_b, h_att, y_f, y_b, proj2, norm_g, h_mem, w_out, ln_g, ln_b)


def _tile(n, pref):
    t = min(n, pref)
    while n % t:
        t //= 2
    return t


def _trunk(x, mem, p):
    b, t, d = x.shape
    m = b * t
    x2 = x.reshape(m, d)
    proj2, dt_raw = _in_proj(x2, p["ln_in_g"], p["ln_in_b"], p["w_main"], p["w_dt"],
                             tm=_tile(m, 1024), tn=512)
    n = proj2.shape[1]
    proj3 = proj2.reshape(b, t, n)
    d_att = ATT_HEADS * ATT_V_DIM
    d_ssm = SSM_HEADS * SSM_HEAD_DIM

    h_att = _attention(proj3, p["diff_lambda"], p["subln_g"], lam_init=p["lam_init"],
                       qb=_tile(t, 2048), kb=_tile(t, 512))

    xbc_act = _conv(proj3, p["conv_w"], p["conv_b"], col0=4 * d_att + d_ssm, tt=_tile(t, 512), tc=1024)
    y_f, y_b = _ssd(xbc_act, dt_raw.reshape(b, t, LANES), p["dtb_row"], p["alog_row"], p["dskip_row"],
                    p["r_f"], p["r_b"])

    kv = _mem_kv(mem, p["w_mem_kv"])
    h_mem = _mem_attn(proj3, kv, q_col0=4 * d_att + d_ssm + p["conv_w"].shape[1], tq=_tile(t, 512))

    y = _out_proj(x2, p["ln_in_g"], p["ln_in_b"], h_att.reshape(m, d_att), y_f.reshape(m, d_ssm),
                  y_b.reshape(m, d_ssm), proj2, p["ssm_norm_g"], h_mem.reshape(m, MEM_HEADS * MEM_HEAD_DIM),
                  p["w_out"], p["ln_g"], p["ln_b"], z_col0=4 * d_att, tm=_tile(m, 256))
    return y.reshape(b, t, d)


def _expansion_matrix(head0):
    rows = lax.broadcasted_iota(jnp.int32, (LANES, SSM_HEADS * SSM_HEAD_DIM), 0)
    cols = lax.broadcasted_iota(jnp.int32, (LANES, SSM_HEADS * SSM_HEAD_DIM), 1)
    return (rows == head0 + cols // SSM_HEAD_DIM).astype(BF16)


def kernel(x_prompt, x_sample, mem_prompt, mem_sample, ln_in_g, ln_in_b, w_in, conv_w, conv_b, dt_bias, a_log, d_skip, ssm_norm_g, diff_lambda, subln_g, w_mem_kv, w_out, ln_g, ln_b):
    assert w_in.shape[0] == DEPTH
    d_att = ATT_HEADS * ATT_V_DIM
    d_ssm = SSM_HEADS * SSM_HEAD_DIM
    d_xbc = conv_w.shape[-1]
    dt0 = 4 * d_att + d_ssm + d_xbc
    dt1 = dt0 + 2 * SSM_HEADS
    row = lambda v: v.reshape(1, -1).astype(F32)
    pad_heads = lambda v: jnp.pad(v.reshape(1, -1).astype(F32), ((0, 0), (0, LANES - 2 * SSM_HEADS)))

    l = 0
    w = w_in[l]
    p = {
        "ln_in_g": row(ln_in_g), "ln_in_b": row(ln_in_b),
        "w_main": jnp.concatenate([w[:, :dt0], w[:, dt1:]], axis=1).astype(BF16),
        "w_dt": jnp.pad(w[:, dt0:dt1], ((0, 0), (0, LANES - 2 * SSM_HEADS))).astype(BF16),
        "conv_w": conv_w[l].astype(F32), "conv_b": row(conv_b[l]),
        "dtb_row": pad_heads(dt_bias[l]), "alog_row": pad_heads(a_log[l]),
        "dskip_row": row(jnp.repeat(d_skip[l], SSM_HEAD_DIM)),
        "r_f": _expansion_matrix(0), "r_b": _expansion_matrix(SSM_HEADS),
        "ssm_norm_g": row(ssm_norm_g[l]),
        "diff_lambda": diff_lambda[l].astype(F32), "subln_g": row(subln_g[l]),
        "lam_init": 0.8 - 0.6 * math.exp(-0.3 * l),
        "w_mem_kv": w_mem_kv[l].astype(BF16), "w_out": w_out[l].astype(BF16),
        "ln_g": row(ln_g[l]), "ln_b": row(ln_b[l]),
    }
    return (_trunk(x_prompt, mem_prompt, p), _trunk(x_sample, mem_sample, p))
```

```python
import functools
import math

import jax
import jax.numpy as jnp
from jax import lax
from jax.experimental import pallas as pl
from jax.experimental.pallas import tpu as pltpu

F32 = jnp.float32
BF16 = jnp.bfloat16

DEPTH = 1
EPS = 1e-5
ATT_HEADS = 8
ATT_HEAD_DIM = 128
ATT_V_DIM = 2 * ATT_HEAD_DIM
SSM_HEADS = 32
SSM_HEAD_DIM = 64
SSM_GROUPS = 4
SSM_STATE = 128
HEADS_PER_GROUP = SSM_HEADS // SSM_GROUPS
GROUP_WIDTH = HEADS_PER_GROUP * SSM_HEAD_DIM
D_CONV = 5
CHUNK = 128
MEM_HEADS = 4
MEM_HEAD_DIM = 128
LANES = 128
LOG2E = 1.4426950408889634
UNDERFLOW_LOG2 = 130.0
NORM_BOUND_SLACK = 1.01
CONV_HALO_ROWS = 16
ALPHA = (2.0 * DEPTH) ** 0.25
VMEM_LIMIT_BYTES = 56 * 1024 * 1024


def _cparams(sem):
    return pltpu.CompilerParams(dimension_semantics=sem, vmem_limit_bytes=VMEM_LIMIT_BYTES)


def _layer_norm(x, g, b):
    mu = jnp.mean(x, axis=-1, keepdims=True)
    xc = x - mu
    var = jnp.mean(xc * xc, axis=-1, keepdims=True)
    return xc * lax.rsqrt(var + EPS) * g + b


def _silu(x):
    return x * jax.nn.sigmoid(x)


def _dot(a, b):
    return jnp.dot(a, b, preferred_element_type=F32)


def _dot_nt(a, b):
    return lax.dot_general(a, b, (((1,), (1,)), ((), ())), preferred_element_type=F32)


def _in_proj_kernel(x_ref, g_ref, b_ref, w_ref, wdt_ref, proj_ref, dt_ref, xn_ref):
    @pl.when(pl.program_id(1) == 0)
    def _():
        xn = _layer_norm(x_ref[...], g_ref[...], b_ref[...]).astype(BF16)
        xn_ref[...] = xn
        dt_ref[...] = _dot(xn, wdt_ref[...])

    proj_ref[...] = _dot(xn_ref[...], w_ref[...]).astype(BF16)


def _in_proj(x2, ln_g, ln_b, w_main, w_dt, *, tm, tn):
    m, d = x2.shape
    n = w_main.shape[1]
    return pl.pallas_call(
        _in_proj_kernel,
        out_shape=(jax.ShapeDtypeStruct((m, n), BF16), jax.ShapeDtypeStruct((m, LANES), F32)),
        grid=(m // tm, n // tn),
        in_specs=[
            pl.BlockSpec((tm, d), lambda i, j: (i, 0)),
            pl.BlockSpec((1, d), lambda i, j: (0, 0)),
            pl.BlockSpec((1, d), lambda i, j: (0, 0)),
            pl.BlockSpec((d, tn), lambda i, j: (0, j)),
            pl.BlockSpec((d, LANES), lambda i, j: (0, 0)),
        ],
        out_specs=(
            pl.BlockSpec((tm, tn), lambda i, j: (i, j)),
            pl.BlockSpec((tm, LANES), lambda i, j: (i, 0)),
        ),
        scratch_shapes=[pltpu.VMEM((tm, d), BF16)],
        compiler_params=_cparams(("parallel", "arbitrary")),
        name="in_proj",
    )(x2, ln_g, ln_b, w_main, w_dt)


def _attn_kernel(dl_ref, subg_ref, q_ref, k_ref, v_ref, g_ref, o_ref, acc_ref, qt_ref, vt_ref, st_ref, kn_ref,
                 *, qb, kb, lam_init):
    h = pl.program_id(1)
    i = pl.program_id(2)
    nblk = k_ref.shape[0] // kb
    ratio = qb // kb
    d = ATT_HEAD_DIM
    c = jnp.exp2(-(h + 1).astype(F32)) * LOG2E

    @pl.when(i == 0)
    def _():
        def tr(jj, kn2):
            start = pl.multiple_of(jj * kb, kb)
            vt_ref[jj] = v_ref[pl.ds(start, kb), :].astype(F32).T.astype(BF16)
            kf = k_ref[pl.ds(start, kb), :].astype(F32)
            sq = kf * kf
            for mp in range(2):
                rows = jnp.sum(sq[:, mp * d:(mp + 1) * d], axis=1, keepdims=True)
                kn2 = jnp.maximum(kn2, jnp.max(rows, axis=0, keepdims=True))
            return kn2
        kn2 = lax.fori_loop(0, nblk, tr, jnp.zeros((1, 1), F32))
        kn_ref[...] = jnp.broadcast_to(kn2, kn_ref.shape)

    def split3(x):
        hi = x.astype(BF16).astype(F32)
        mid = (x - hi).astype(BF16).astype(F32)
        return hi, mid, x - hi - mid

    lane = lax.broadcasted_iota(jnp.int32, (kb, d), 1)
    ku = split3(c * lax.broadcasted_iota(jnp.int32, (kb, d), 0).astype(F32))
    aug_k = jnp.where(lane < 3, -1.0, jnp.where(lane == 3, ku[0], jnp.where(lane == 4, ku[1],
                      jnp.where(lane == 5, ku[2], 0.0)))).astype(BF16)
    row = lax.broadcasted_iota(jnp.int32, (d, qb), 0)
    qr = split3(c * lax.broadcasted_iota(jnp.int32, (d, qb), 1).astype(F32))
    aug_qt = jnp.where(row == 0, qr[0], jnp.where(row == 1, qr[1], jnp.where(row == 2, qr[2],
                       jnp.where(row < 6, 1.0, 0.0))))
    q = q_ref[...].astype(F32) * (LOG2E / math.sqrt(d))
    qn2 = jnp.zeros((1, 1), F32)
    for mp in range(2):
        qtf = q[:, mp * d:(mp + 1) * d].T
        qn2 = jnp.maximum(qn2, jnp.max(jnp.sum(qtf * qtf, axis=0, keepdims=True), axis=1, keepdims=True))
        qt = qtf.astype(BF16)
        qt_ref[mp] = jnp.concatenate([qt, aug_qt.astype(BF16)], axis=0)
        qt_ref[2 + mp] = jnp.concatenate([qt, (-aug_qt).astype(BF16)], axis=0)

    bound = jnp.sqrt(qn2 * kn_ref[0:1, 0:1]) * NORM_BOUND_SLACK
    reach = (2.0 * bound + UNDERFLOW_LOG2) / c
    q0 = (i * qb).astype(F32)
    seq_last = float(k_ref.shape[0] - 1)
    first = jnp.floor(jnp.maximum(q0 - reach, 0.0) * (1.0 / kb)).astype(jnp.int32)[0, 0]
    last = jnp.floor(jnp.minimum(q0 + (qb - 1) + reach, seq_last) * (1.0 / kb)).astype(jnp.int32)[0, 0]
    j_lo = jnp.minimum(first, i * ratio)
    j_hi = jnp.maximum(last + 1, (i + 1) * ratio)

    acc_ref[...] = jnp.zeros_like(acc_ref)
    qw = st_ref.shape[1]
    nq = qb // qw
    chains = [(mp, hf) for mp in range(2) for hf in range(nq)]

    def key_operand(j, mp):
        start = pl.multiple_of(j * kb, kb)
        return jnp.concatenate([k_ref[pl.ds(start, kb), mp * d:(mp + 1) * d], aug_k], axis=1)

    def step(j, carry, variant, delta, overlap, next_variant):
        vt = vt_ref[j]
        kp = [key_operand(j, mp) for mp in range(2)]

        def scores(ch):
            mp, hf = ch
            return _dot(kp[mp], qt_ref[2 * variant + mp, :, hf * qw:(hf + 1) * qw])

        def finish(ch, st):
            mp, hf = ch
            if overlap is not None:
                ahead = (lax.broadcasted_iota(jnp.int32, (kb, qw), 0) - lax.broadcasted_iota(jnp.int32, (kb, qw), 1)
                         + (overlap - hf * qw))
                st = st + (-2.0 * c) * jnp.maximum(ahead, 0).astype(F32)
            m_old, l_old = carry[2 * mp][:, hf * qw:(hf + 1) * qw], carry[2 * mp + 1][:, hf * qw:(hf + 1) * qw]
            m_new = jnp.maximum(m_old, jnp.max(st, axis=0, keepdims=True) + delta)
            a = jnp.exp2(m_old - m_new)
            pt = jnp.exp2(st - (m_new - delta))
            l_new = a * l_old + jnp.sum(pt, axis=0, keepdims=True)
            acc_ref[mp, :, hf * qw:(hf + 1) * qw] = (a * acc_ref[mp, :, hf * qw:(hf + 1) * qw]
                                                     + _dot(vt, pt.astype(BF16)))
            return m_new, l_new

        stats = {}
        pending = st_ref[...]
        for n, ch in enumerate(chains):
            st = pending
            if n + 1 < len(chains):
                pending = scores(chains[n + 1])
            else:
                jn = jnp.minimum(j + 1, nblk - 1)
                st_ref[...] = _dot(key_operand(jn, 0), qt_ref[2 * next_variant, :, 0:qw])
            stats[ch] = finish(ch, st)
        new = []
        for mp in range(2):
            new.append(jnp.concatenate([stats[(mp, hf)][0] for hf in range(nq)], axis=1))
            new.append(jnp.concatenate([stats[(mp, hf)][1] for hf in range(nq)], axis=1))
        return tuple(new)

    def below(j, carry):
        return step(j, carry, 0, -c * (i * qb - j * kb).astype(F32), None, 0)

    def above(j, carry):
        return step(j, carry, 1, -c * (j * kb - i * qb).astype(F32), None, 1)

    neg = jnp.full((1, qb), -1e30, F32)
    zero = jnp.zeros((1, qb), F32)
    st_ref[...] = _dot(key_operand(j_lo, 0), qt_ref[0, :, 0:qw])
    carry = lax.fori_loop(j_lo, i * ratio, below, (neg, zero, neg, zero))
    for jj in range(ratio):
        carry = step(i * ratio + jj, carry, 0, c * float(jj * kb), jj * kb, 0 if jj + 1 < ratio else 1)
    _, l0, _, l1 = lax.fori_loop((i + 1) * ratio, j_hi, above, carry)

    dl = dl_ref[...]
    lam = (jnp.exp(jnp.sum(dl[0:1] * dl[1:2], axis=-1, keepdims=True))
           - jnp.exp(jnp.sum(dl[2:3] * dl[3:4], axis=-1, keepdims=True)) + lam_init)
    out = (acc_ref[0] / l0 - lam * (acc_ref[1] / l1)).T
    ms = jnp.mean(out * out, axis=-1, keepdims=True)
    out = out * lax.rsqrt(ms + EPS) * subg_ref[...] * (1.0 - lam_init)
    o_ref[...] = (out * _silu(g_ref[...].astype(F32))).astype(BF16)


def _attention(proj3, diff_lambda, subln_g, *, lam_init, qb, kb):
    b, t, _ = proj3.shape
    hh = ATT_HEADS
    tq = qb
    kern = functools.partial(_attn_kernel, qb=qb, kb=kb, lam_init=lam_init)
    return pl.pallas_call(
        kern,
        out_shape=jax.ShapeDtypeStruct((b, t, hh * ATT_V_DIM), BF16),
        grid=(b, hh, t // tq),
        in_specs=[
            pl.BlockSpec((4, ATT_HEAD_DIM), lambda bi, h, i: (0, 0)),
            pl.BlockSpec((1, ATT_V_DIM), lambda bi, h, i: (0, 0)),
            pl.BlockSpec((None, tq, ATT_V_DIM), lambda bi, h, i: (bi, i, h)),
            pl.BlockSpec((None, t, ATT_V_DIM), lambda bi, h, i: (bi, 0, hh + h)),
            pl.BlockSpec((None, t, ATT_V_DIM), lambda bi, h, i: (bi, 0, 2 * hh + h)),
            pl.BlockSpec((None, tq, ATT_V_DIM), lambda bi, h, i: (bi, i, 3 * hh + h)),
        ],
        out_specs=pl.BlockSpec((None, tq, ATT_V_DIM), lambda bi, h, i: (bi, i, h)),
        scratch_shapes=[
            pltpu.VMEM((2, ATT_V_DIM, qb), F32),
            pltpu.VMEM((4, 2 * ATT_HEAD_DIM, qb), BF16),
            pltpu.VMEM((t // kb, ATT_V_DIM, kb), BF16),
            pltpu.VMEM((kb, min(qb, 2 * LANES)), F32),
            pltpu.VMEM((8, LANES), F32),
        ],
        compiler_params=_cparams(("parallel", "parallel", "arbitrary")),
        name="diff_attn",
    )(diff_lambda, subln_g, proj3, proj3, proj3, proj3)


def _conv_kernel(cur_ref, prev_ref, next_ref, w_ref, b_ref, o_ref, u_ref, *, tt):
    i = pl.program_id(1)
    halo = CONV_HALO_ROWS
    pad = (D_CONV - 1) // 2
    prev = prev_ref[...].astype(F32)
    nxt = next_ref[...].astype(F32)
    u_ref[0:halo, :] = jnp.where(i > 0, prev, 0.0)
    u_ref[halo:halo + tt, :] = cur_ref[...].astype(F32)
    u_ref[halo + tt:2 * halo + tt, :] = jnp.where(i < pl.num_programs(1) - 1, nxt, 0.0)
    acc = u_ref[pl.ds(halo - pad, tt), :] * w_ref[0:1, :]
    for k in range(1, D_CONV):
        acc = acc + u_ref[pl.ds(halo - pad + k, tt), :] * w_ref[k:k + 1, :]
    o_ref[...] = _silu(acc + b_ref[...]).astype(BF16)


def _conv(proj3, conv_w, conv_b, *, col0, tt, tc):
    b, t, _ = proj3.shape
    width = conv_w.shape[1]
    halo = CONV_HALO_ROWS
    cb0 = col0 // tc
    rpb = tt // halo
    last = t // halo - 1
    kern = functools.partial(_conv_kernel, tt=tt)
    return pl.pallas_call(
        kern,
        out_shape=jax.ShapeDtypeStruct((b, t, width), BF16),
        grid=(b, t // tt, width // tc),
        in_specs=[
            pl.BlockSpec((None, tt, tc), lambda bi, i, c: (bi, i, cb0 + c)),
            pl.BlockSpec((None, halo, tc), lambda bi, i, c: (bi, jnp.maximum(i * rpb - 1, 0), cb0 + c)),
            pl.BlockSpec((None, halo, tc), lambda bi, i, c: (bi, jnp.minimum((i + 1) * rpb, last), cb0 + c)),
            pl.BlockSpec((D_CONV, tc), lambda bi, i, c: (0, c)),
            pl.BlockSpec((1, tc), lambda bi, i, c: (0, c)),
        ],
        out_specs=pl.BlockSpec((None, tt, tc), lambda bi, i, c: (bi, i, c)),
        scratch_shapes=[pltpu.VMEM((tt + 2 * halo, tc), F32)],
        compiler_params=_cparams(("parallel", "parallel", "parallel")),
        name="conv_silu",
    )(proj3, proj3, proj3, conv_w, conv_b)


def _cumsum_rows(x):
    row = lax.broadcasted_iota(jnp.int32, x.shape, 0)
    sh = 1
    while sh < x.shape[0]:
        x = x + jnp.where(row >= sh, pltpu.roll(x, sh, 0), 0.0)
        sh *= 2
    return x


def _ssd_direction(xa_ref, dtr_ref, dtb_ref, alog_ref, r_ref, s_ref, y_ref, dskip_ref, *, head0, backward):
    L = CHUNK
    d_ssm = SSM_HEADS * SSM_HEAD_DIM
    n_bc = SSM_GROUPS * SSM_STATE
    dt = jax.nn.softplus(dtr_ref[...] + dtb_ref[...])
    a = dt * (-jnp.exp(alog_ref[...]))
    p = _cumsum_rows(a)
    tot = p[L - 1:L, :]
    ac = (tot - p + a) if backward else p
    eo = jnp.exp(ac)
    ds = jnp.exp(tot - ac)
    cd = jnp.broadcast_to(jnp.exp(tot), (8, LANES))
    stack = jnp.concatenate([dt, eo, ds, cd], axis=0)
    ex = _dot(stack.astype(BF16), r_ref[...])
    dtx, eox, dsx, cdx = ex[0:L], ex[L:2 * L], ex[2 * L:3 * L], ex[3 * L:3 * L + 1]

    xs = xa_ref[:, 0:d_ssm].astype(F32)
    xd = xs * dtx
    xdb = xd.astype(BF16)
    wst = (xd * dsx).astype(BF16)
    ac_t = ac.T
    row = lax.broadcasted_iota(jnp.int32, (L, L), 0)
    col = lax.broadcasted_iota(jnp.int32, (L, L), 1)
    mask = (row <= col) if backward else (row >= col)
    lane_lo = col < SSM_HEAD_DIM
    zero_b = jnp.zeros((L, L), BF16)

    for g in range(SSM_GROUPS):
        c0 = g * GROUP_WIDTH
        bg = xa_ref[:, d_ssm + g * SSM_STATE:d_ssm + (g + 1) * SSM_STATE]
        cg = xa_ref[:, d_ssm + n_bc + g * SSM_STATE:d_ssm + n_bc + (g + 1) * SSM_STATE]
        cb = _dot_nt(cg, bg)
        st = s_ref[g]
        y = _dot(cg, st.astype(BF16)) * eox[:, c0:c0 + GROUP_WIDTH]
        if dskip_ref is not None:
            y = y + xs[:, c0:c0 + GROUP_WIDTH] * dskip_ref[:, c0:c0 + GROUP_WIDTH]
        pieces = []
        for pp in range(HEADS_PER_GROUP // 2):
            ms = []
            for e in (2 * pp, 2 * pp + 1):
                hc = head0 + g * HEADS_PER_GROUP + e
                seg = ac[:, hc:hc + 1] - ac_t[hc:hc + 1, :]
                lm = jnp.exp(jnp.where(mask, seg, -jnp.inf))
                ms.append((cb * lm).astype(BF16))
            xpair = xdb[:, c0 + pp * LANES:c0 + (pp + 1) * LANES]
            rhs = jnp.concatenate([jnp.where(lane_lo, xpair, zero_b), jnp.where(lane_lo, zero_b, xpair)], axis=0)
            pieces.append(_dot(jnp.concatenate(ms, axis=1), rhs))
        y = y + jnp.concatenate(pieces, axis=1)
        y_ref[:, c0:c0 + GROUP_WIDTH] = y.astype(BF16)
        bt = bg.astype(F32).T.astype(BF16)
        s_ref[g] = st * cdx[:, c0:c0 + GROUP_WIDTH] + _dot(bt, wst[:, c0:c0 + GROUP_WIDTH])


def _ssd_kernel(dtb_ref, alog_ref, dskip_ref, rf_ref, rb_ref, xaf_ref, dtf_ref, xab_ref, dtbk_ref,
                yf_ref, yb_ref, sf_ref, sb_ref):
    @pl.when(pl.program_id(1) == 0)
    def _():
        sf_ref[...] = jnp.zeros_like(sf_ref)
        sb_ref[...] = jnp.zeros_like(sb_ref)

    _ssd_direction(xaf_ref, dtf_ref, dtb_ref, alog_ref, rf_ref, sf_ref, yf_ref, dskip_ref,
                   head0=0, backward=False)
    _ssd_direction(xab_ref, dtbk_ref, dtb_ref, alog_ref, rb_ref, sb_ref, yb_ref, None,
                   head0=SSM_HEADS, backward=True)


def _ssd(xbc_act, dt_raw3, dtb_row, alog_row, dskip_row, r_f, r_b):
    b, t, wa = xbc_act.shape
    nc = t // CHUNK
    d_ssm = SSM_HEADS * SSM_HEAD_DIM
    const = lambda bi, c: (0, 0)
    return pl.pallas_call(
        _ssd_kernel,
        out_shape=(jax.ShapeDtypeStruct((b, t, d_ssm), BF16), jax.ShapeDtypeStruct((b, t, d_ssm), BF16)),
        grid=(b, nc),
        in_specs=[
            pl.BlockSpec((1, LANES), const),
            pl.BlockSpec((1, LANES), const),
            pl.BlockSpec((1, d_ssm), const),
            pl.BlockSpec((LANES, d_ssm), const),
            pl.BlockSpec((LANES, d_ssm), const),
            pl.BlockSpec((None, CHUNK, wa), lambda bi, c: (bi, c, 0)),
            pl.BlockSpec((None, CHUNK, LANES), lambda bi, c: (bi, c, 0)),
            pl.BlockSpec((None, CHUNK, wa), lambda bi, c: (bi, nc - 1 - c, 0)),
            pl.BlockSpec((None, CHUNK, LANES), lambda bi, c: (bi, nc - 1 - c, 0)),
        ],
        out_specs=(
            pl.BlockSpec((None, CHUNK, d_ssm), lambda bi, c: (bi, c, 0)),
            pl.BlockSpec((None, CHUNK, d_ssm), lambda bi, c: (bi, nc - 1 - c, 0)),
        ),
        scratch_shapes=[
            pltpu.VMEM((SSM_GROUPS, SSM_STATE, GROUP_WIDTH), F32),
            pltpu.VMEM((SSM_GROUPS, SSM_STATE, GROUP_WIDTH), F32),
        ],
        compiler_params=_cparams(("parallel", "arbitrary")),
        name="ssd_scan",
    )(dtb_row, alog_row, dskip_row, r_f, r_b, xbc_act, dt_raw3, xbc_act, dt_raw3)


def _mem_kv_kernel(mem_ref, w_ref, o_ref):
    o_ref[...] = _dot(mem_ref[...].astype(BF16), w_ref[...]).astype(BF16)


def _mem_kv(mem, w_kv):
    b, mt, d = mem.shape
    n = w_kv.shape[1]
    return pl.pallas_call(
        _mem_kv_kernel,
        out_shape=jax.ShapeDtypeStruct((b, mt, n), BF16),
        grid=(b,),
        in_specs=[
            pl.BlockSpec((None, mt, d), lambda bi: (bi, 0, 0)),
            pl.BlockSpec((d, n), lambda bi: (0, 0)),
        ],
        out_specs=pl.BlockSpec((None, mt, n), lambda bi: (bi, 0, 0)),
        compiler_params=_cparams(("parallel",)),
        name="mem_kv_proj",
    )(mem, w_kv)


def _mem_attn_kernel(q_ref, g_ref, kv_ref, o_ref):
    d = MEM_HEAD_DIM
    d_mem = MEM_HEADS * d
    scale = 1.0 / math.sqrt(d)
    for hh in range(MEM_HEADS):
        q = q_ref[:, hh * d:(hh + 1) * d]
        k = kv_ref[:, hh * d:(hh + 1) * d]
        v = kv_ref[:, d_mem + hh * d:d_mem + (hh + 1) * d]
        s = _dot_nt(q, k) * scale
        p = jnp.exp(s - jnp.max(s, axis=-1, keepdims=True))
        o = _dot(p.astype(BF16), v) / jnp.sum(p, axis=-1, keepdims=True)
        o_ref[:, hh * d:(hh + 1) * d] = (o * _silu(g_ref[:, hh * d:(hh + 1) * d].astype(F32))).astype(BF16)


def _mem_attn(proj3, kv, *, q_col0, tq):
    b, t, _ = proj3.shape
    mt, n = kv.shape[1:]
    d_mem = MEM_HEADS * MEM_HEAD_DIM
    qb = q_col0 // d_mem
    return pl.pallas_call(
        _mem_attn_kernel,
        out_shape=jax.ShapeDtypeStruct((b, t, d_mem), BF16),
        grid=(b, t // tq),
        in_specs=[
            pl.BlockSpec((None, tq, d_mem), lambda bi, i: (bi, i, qb)),
            pl.BlockSpec((None, tq, d_mem), lambda bi, i: (bi, i, qb + 1)),
            pl.BlockSpec((None, mt, n), lambda bi, i: (bi, 0, 0)),
        ],
        out_specs=pl.BlockSpec((None, tq, d_mem), lambda bi, i: (bi, i, 0)),
        compiler_params=_cparams(("parallel", "parallel")),
        name="mem_attn",
    )(proj3, proj3, kv)


def _out_proj_kernel(x_ref, gi_ref, bi_ref, ha_ref, yf_ref, yb_ref, z_ref, ng_ref, hm_ref, w_ref, g_ref, b_ref,
                     o_ref):
    da = ha_ref.shape[1]
    dsm = yf_ref.shape[1]
    out = _dot(ha_ref[...], w_ref[0:da, :])
    out = out + _dot(hm_ref[...], w_ref[da + dsm:, :])
    for g in range(SSM_GROUPS):
        c0 = g * GROUP_WIDTH
        yg = ((yf_ref[:, c0:c0 + GROUP_WIDTH].astype(F32) + yb_ref[:, c0:c0 + GROUP_WIDTH].astype(F32))
              * _silu(z_ref[:, c0:c0 + GROUP_WIDTH].astype(F32)))
        ms = jnp.mean(yg * yg, axis=-1, keepdims=True)
        hg = (yg * lax.rsqrt(ms + EPS) * ng_ref[:, c0:c0 + GROUP_WIDTH]).astype(BF16)
        out = out + _dot(hg, w_ref[da + c0:da + c0 + GROUP_WIDTH, :])
    xn = _layer_norm(x_ref[...], gi_ref[...], bi_ref[...])
    o_ref[...] = _layer_norm(ALPHA * xn + out, g_ref[...], b_ref[...])


def _out_proj(x2, ln_in_g, ln_in_b, h_att, y_f, y_b, proj2, norm_g, h_mem, w_out, ln_g, ln_b, *, z_col0, tm):
    m, d = x2.shape
    da, dsm, dm = h_att.shape[1], y_f.shape[1], h_mem.shape[1]
    zb = z_col0 // dsm
    row = lambda i: (i, 0)
    const = lambda i: (0, 0)
    return pl.pallas_call(
        _out_proj_kernel,
        out_shape=jax.ShapeDtypeStruct((m, d), F32),
        grid=(m // tm,),
        in_specs=[
            pl.BlockSpec((tm, d), row),
            pl.BlockSpec((1, d), const),
            pl.BlockSpec((1, d), const),
            pl.BlockSpec((tm, da), row),
            pl.BlockSpec((tm, dsm), row),
            pl.BlockSpec((tm, dsm), row),
            pl.BlockSpec((tm, dsm), lambda i: (i, zb)),
            pl.BlockSpec((1, dsm), const),
            pl.BlockSpec((tm, dm), row),
            pl.BlockSpec(w_out.shape, const, pipeline_mode=pl.Buffered(1)),
            pl.BlockSpec((1, d), const),
            pl.BlockSpec((1, d), const),
        ],
        out_specs=pl.BlockSpec((tm, d), row),
        compiler_params=_cparams(("parallel",)),
        name="out_proj_ln",
    )(x2, ln_in_g, ln_in_b, h_att, y_f, y_b, proj2, norm_g, h_mem, w_out, ln_g, ln_b)


def _tile(n, pref):
    t = min(n, pref)
    while n % t:
        t //= 2
    return t


def _trunk(x, mem, p):
    b, t, d = x.shape
    m = b * t
    x2 = x.reshape(m, d)
    proj2, dt_raw = _in_proj(x2, p["ln_in_g"], p["ln_in_b"], p["w_main"], p["w_dt"],
                             tm=_tile(m, 1024), tn=512)
    n = proj2.shape[1]
    proj3 = proj2.reshape(b, t, n)
    d_att = ATT_HEADS * ATT_V_DIM
    d_ssm = SSM_HEADS * SSM_HEAD_DIM

    h_att = _attention(proj3, p["diff_lambda"], p["subln_g"], lam_init=p["lam_init"],
                       qb=_tile(t, 2048), kb=_tile(t, 512))

    xbc_act = _conv(proj3, p["conv_w"], p["conv_b"], col0=4 * d_att + d_ssm, tt=_tile(t, 512), tc=1024)
    y_f, y_b = _ssd(xbc_act, dt_raw.reshape(b, t, LANES), p["dtb_row"], p["alog_row"], p["dskip_row"],
                    p["r_f"], p["r_b"])

    kv = _mem_kv(mem, p["w_mem_kv"])
    h_mem = _mem_attn(proj3, kv, q_col0=4 * d_att + d_ssm + p["conv_w"].shape[1], tq=_tile(t, 512))

    y = _out_proj(x2, p["ln_in_g"], p["ln_in_b"], h_att.reshape(m, d_att), y_f.reshape(m, d_ssm),
                  y_b.reshape(m, d_ssm), proj2, p["ssm_norm_g"], h_mem.reshape(m, MEM_HEADS * MEM_HEAD_DIM),
                  p["w_out"], p["ln_g"], p["ln_b"], z_col0=4 * d_att, tm=_tile(m, 256))
    return y.reshape(b, t, d)


def _expansion_matrix(head0):
    rows = lax.broadcasted_iota(jnp.int32, (LANES, SSM_HEADS * SSM_HEAD_DIM), 0)
    cols = lax.broadcasted_iota(jnp.int32, (LANES, SSM_HEADS * SSM_HEAD_DIM), 1)
    return (rows == head0 + cols // SSM_HEAD_DIM).astype(BF16)


def kernel(x_prompt, x_sample, mem_prompt, mem_sample, ln_in_g, ln_in_b, w_in, conv_w, conv_b, dt_bias, a_log, d_skip, ssm_norm_g, diff_lambda, subln_g, w_mem_kv, w_out, ln_g, ln_b):
    assert w_in.shape[0] == DEPTH
    d_att = ATT_HEADS * ATT_V_DIM
    d_ssm = SSM_HEADS * SSM_HEAD_DIM
    d_xbc = conv_w.shape[-1]
    dt0 = 4 * d_att + d_ssm + d_xbc
    dt1 = dt0 + 2 * SSM_HEADS
    row = lambda v: v.reshape(1, -1).astype(F32)
    pad_heads = lambda v: jnp.pad(v.reshape(1, -1).astype(F32), ((0, 0), (0, LANES - 2 * SSM_HEADS)))

    l = 0
    w = w_in[l]
    p = {
        "ln_in_g": row(ln_in_g), "ln_in_b": row(ln_in_b),
        "w_main": jnp.concatenate([w[:, :dt0], w[:, dt1:]], axis=1).astype(BF16),
        "w_dt": jnp.pad(w[:, dt0:dt1], ((0, 0), (0, LANES - 2 * SSM_HEADS))).astype(BF16),
        "conv_w": conv_w[l].astype(F32), "conv_b": row(conv_b[l]),
        "dtb_row": pad_heads(dt_bias[l]), "alog_row": pad_heads(a_log[l]),
        "dskip_row": row(jnp.repeat(d_skip[l], SSM_HEAD_DIM)),
        "r_f": _expansion_matrix(0), "r_b": _expansion_matrix(SSM_HEADS),
        "ssm_norm_g": row(ssm_norm_g[l]),
        "diff_lambda": diff_lambda[l].astype(F32), "subln_g": row(subln_g[l]),
        "lam_init": 0.8 - 0.6 * math.exp(-0.3 * l),
        "w_mem_kv": w_mem_kv[l].astype(BF16), "w_out": w_out[l].astype(BF16),
        "ln_g": row(ln_g[l]), "ln_b": row(ln_b[l]),
    }
    return (_trunk(x_prompt, mem_prompt, p), _trunk(x_sample, mem_sample, p))
```

```python
import functools
import math

import jax
import jax.numpy as jnp
from jax import lax
from jax.experimental import pallas as pl
from jax.experimental.pallas import tpu as pltpu

F32 = jnp.float32
BF16 = jnp.bfloat16

DEPTH = 1
EPS = 1e-5
ATT_HEADS = 8
ATT_HEAD_DIM = 128
ATT_V_DIM = 2 * ATT_HEAD_DIM
SSM_HEADS = 32
SSM_HEAD_DIM = 64
SSM_GROUPS = 4
SSM_STATE = 128
HEADS_PER_GROUP = SSM_HEADS // SSM_GROUPS
GROUP_WIDTH = HEADS_PER_GROUP * SSM_HEAD_DIM
D_CONV = 5
CHUNK = 128
MEM_HEADS = 4
MEM_HEAD_DIM = 128
LANES = 128
LOG2E = 1.4426950408889634
UNDERFLOW_LOG2 = 130.0
NORM_BOUND_SLACK = 1.01
FIXED_MAX_RANGE = 100.0
CONV_HALO_ROWS = 16
ALPHA = (2.0 * DEPTH) ** 0.25
VMEM_LIMIT_BYTES = 56 * 1024 * 1024


def _cparams(sem):
    return pltpu.CompilerParams(dimension_semantics=sem, vmem_limit_bytes=VMEM_LIMIT_BYTES)


def _layer_norm(x, g, b):
    mu = jnp.mean(x, axis=-1, keepdims=True)
    xc = x - mu
    var = jnp.mean(xc * xc, axis=-1, keepdims=True)
    return xc * lax.rsqrt(var + EPS) * g + b


def _silu(x):
    return x * jax.nn.sigmoid(x)


def _dot(a, b):
    return jnp.dot(a, b, preferred_element_type=F32)


def _dot_nt(a, b):
    return lax.dot_general(a, b, (((1,), (1,)), ((), ())), preferred_element_type=F32)


def _in_proj_kernel(x_ref, g_ref, b_ref, w_ref, wdt_ref, proj_ref, dt_ref, xn_ref):
    @pl.when(pl.program_id(1) == 0)
    def _():
        xn = _layer_norm(x_ref[...], g_ref[...], b_ref[...]).astype(BF16)
        xn_ref[...] = xn
        dt_ref[...] = _dot(xn, wdt_ref[...])

    proj_ref[...] = _dot(xn_ref[...], w_ref[...]).astype(BF16)


def _in_proj(x2, ln_g, ln_b, w_main, w_dt, *, tm, tn):
    m, d = x2.shape
    n = w_main.shape[1]
    return pl.pallas_call(
        _in_proj_kernel,
        out_shape=(jax.ShapeDtypeStruct((m, n), BF16), jax.ShapeDtypeStruct((m, LANES), F32)),
        grid=(m // tm, n // tn),
        in_specs=[
            pl.BlockSpec((tm, d), lambda i, j: (i, 0)),
            pl.BlockSpec((1, d), lambda i, j: (0, 0)),
            pl.BlockSpec((1, d), lambda i, j: (0, 0)),
            pl.BlockSpec((d, tn), lambda i, j: (0, j)),
            pl.BlockSpec((d, LANES), lambda i, j: (0, 0)),
        ],
        out_specs=(
            pl.BlockSpec((tm, tn), lambda i, j: (i, j)),
            pl.BlockSpec((tm, LANES), lambda i, j: (i, 0)),
        ),
        scratch_shapes=[pltpu.VMEM((tm, d), BF16)],
        compiler_params=_cparams(("parallel", "arbitrary")),
        name="in_proj",
    )(x2, ln_g, ln_b, w_main, w_dt)


def _attn_kernel(dl_ref, subg_ref, q_ref, k_ref, v_ref, g_ref, o_ref, acc_ref, qt_ref, vt_ref, st_ref, kn_ref, l_ref,
                 *, qb, kb, lam_init):
    h = pl.program_id(1)
    i = pl.program_id(2)
    nblk = k_ref.shape[0] // kb
    ratio = qb // kb
    d = ATT_HEAD_DIM
    c = jnp.exp2(-(h + 1).astype(F32)) * LOG2E

    @pl.when(i == 0)
    def _():
        def tr(jj, kn2):
            start = pl.multiple_of(jj * kb, kb)
            vt_ref[jj] = v_ref[pl.ds(start, kb), :].astype(F32).T.astype(BF16)
            kf = k_ref[pl.ds(start, kb), :].astype(F32)
            sq = kf * kf
            for mp in range(2):
                rows = jnp.sum(sq[:, mp * d:(mp + 1) * d], axis=1, keepdims=True)
                kn2 = jnp.maximum(kn2, jnp.max(rows, axis=0, keepdims=True))
            return kn2
        kn2 = lax.fori_loop(0, nblk, tr, jnp.zeros((1, 1), F32))
        kn_ref[...] = jnp.broadcast_to(kn2, kn_ref.shape)

    def split3(x):
        hi = x.astype(BF16).astype(F32)
        mid = (x - hi).astype(BF16).astype(F32)
        return hi, mid, x - hi - mid

    q = q_ref[...].astype(F32) * (LOG2E / math.sqrt(d))
    qt, qrow2 = [], []
    for mp in range(2):
        qtf = q[:, mp * d:(mp + 1) * d].T
        qrow2.append(jnp.sum(qtf * qtf, axis=0, keepdims=True))
        qt.append(qtf.astype(BF16))
    qn2 = jnp.max(jnp.maximum(qrow2[0], qrow2[1]), axis=1, keepdims=True)
    kmax = jnp.sqrt(kn_ref[0:1, 0:1]) * NORM_BOUND_SLACK

    bound = jnp.sqrt(qn2) * kmax
    reach = (2.0 * bound + UNDERFLOW_LOG2) / c
    q0 = (i * qb).astype(F32)
    seq_last = float(k_ref.shape[0] - 1)
    first = jnp.floor(jnp.maximum(q0 - reach, 0.0) * (1.0 / kb)).astype(jnp.int32)[0, 0]
    last = jnp.floor(jnp.minimum(q0 + (qb - 1) + reach, seq_last) * (1.0 / kb)).astype(jnp.int32)[0, 0]
    j_lo = jnp.minimum(first, i * ratio)
    j_hi = jnp.maximum(last + 1, (i + 1) * ratio)
    fixed_ok = (2.0 * bound <= FIXED_MAX_RANGE).astype(jnp.int32)[0, 0]

    acc_ref[...] = jnp.zeros_like(acc_ref)
    qw = st_ref.shape[1]
    nq = qb // qw
    chains = [(mp, hf) for mp in range(2) for hf in range(nq)]
    row = lax.broadcasted_iota(jnp.int32, (d, qb), 0)
    lane = lax.broadcasted_iota(jnp.int32, (kb, d), 1)
    cr = c * lax.broadcasted_iota(jnp.int32, (d, qb), 1).astype(F32)
    upos = lax.broadcasted_iota(jnp.int32, (kb, d), 0).astype(F32)

    def query_aug(x, ones):
        x = split3(x)
        return jnp.where(row == 0, x[0], jnp.where(row == 1, x[1], jnp.where(row == 2, x[2],
                         jnp.where(row < 6, ones, 0.0)))).astype(BF16)

    def key_aug(y):
        y = split3(y)
        return jnp.where(lane < 3, -1.0, jnp.where(lane == 3, y[0], jnp.where(lane == 4, y[1],
                         jnp.where(lane == 5, y[2], 0.0)))).astype(BF16)

    def key_operand(j, mp, aug):
        start = pl.multiple_of(j * kb, kb)
        return jnp.concatenate([k_ref[pl.ds(start, kb), mp * d:(mp + 1) * d], aug], axis=1)

    def overlap_term(overlap, hf):
        ahead = (lax.broadcasted_iota(jnp.int32, (kb, qw), 0) - lax.broadcasted_iota(jnp.int32, (kb, qw), 1)
                 + (overlap - hf * qw))
        return (-2.0 * c) * jnp.maximum(ahead, 0).astype(F32)

    def run_chains(j, variant, next_variant, aug_of, finish):
        kp = [key_operand(j, mp, aug_of(j)) for mp in range(2)]
        results = {}
        pending = st_ref[...]
        for n, (mp, hf) in enumerate(chains):
            st = pending
            if n + 1 < len(chains):
                mp1, hf1 = chains[n + 1]
                pending = _dot(kp[mp1], qt_ref[2 * variant + mp1, :, hf1 * qw:(hf1 + 1) * qw])
            else:
                jn = jnp.minimum(j + 1, nblk - 1)
                st_ref[...] = _dot(key_operand(jn, 0, aug_of(jn)), qt_ref[2 * next_variant, :, 0:qw])
            results[(mp, hf)] = finish(mp, hf, st)
        return results

    def gather(results, idx):
        return [jnp.concatenate([results[(mp, hf)][idx] for hf in range(nq)], axis=1) for mp in range(2)]

    def fixed_path():
        for mp in range(2):
            m_ub = jnp.sqrt(qrow2[mp]) * kmax
            qt_ref[mp] = jnp.concatenate([qt[mp], query_aug(m_ub + cr, 1.0)], axis=0)
            qt_ref[2 + mp] = jnp.concatenate([qt[mp], query_aug(m_ub - cr, -1.0)], axis=0)

        def aug_of(j):
            return key_aug(c * ((j * kb - i * qb).astype(F32) + upos))

        def step(j, carry, variant, overlap, next_variant):
            vt = vt_ref[j]

            def finish(mp, hf, st):
                if overlap is not None:
                    st = st + overlap_term(overlap, hf)
                pt = jnp.exp2(st)
                acc_ref[mp, :, hf * qw:(hf + 1) * qw] += _dot(vt, pt.astype(BF16))
                return (jnp.sum(pt, axis=0, keepdims=True),)

            sums = gather(run_chains(j, variant, next_variant, aug_of, finish), 0)
            return (carry[0] + sums[0], carry[1] + sums[1])

        zero = jnp.zeros((1, qb), F32)
        st_ref[...] = _dot(key_operand(j_lo, 0, aug_of(j_lo)), qt_ref[0, :, 0:qw])
        carry = lax.fori_loop(j_lo, i * ratio, lambda j, cy: step(j, cy, 0, None, 0), (zero, zero))
        for jj in range(ratio):
            carry = step(i * ratio + jj, carry, 0, jj * kb, 0 if jj + 1 < ratio else 1)
        carry = lax.fori_loop((i + 1) * ratio, j_hi, lambda j, cy: step(j, cy, 1, None, 1), carry)
        l_ref[0] = carry[0]
        l_ref[1] = carry[1]

    def running_path():
        aug_k = key_aug(c * upos)
        for mp in range(2):
            qt_ref[mp] = jnp.concatenate([qt[mp], query_aug(cr, 1.0)], axis=0)
            qt_ref[2 + mp] = jnp.concatenate([qt[mp], query_aug(-cr, -1.0)], axis=0)

        def step(j, carry, variant, delta, overlap, next_variant):
            vt = vt_ref[j]

            def finish(mp, hf, st):
                if overlap is not None:
                    st = st + overlap_term(overlap, hf)
                m_old, l_old = carry[2 * mp][:, hf * qw:(hf + 1) * qw], carry[2 * mp + 1][:, hf * qw:(hf + 1) * qw]
                m_new = jnp.maximum(m_old, jnp.max(st, axis=0, keepdims=True) + delta)
                a = jnp.exp2(m_old - m_new)
                pt = jnp.exp2(st - (m_new - delta))
                l_new = a * l_old + jnp.sum(pt, axis=0, keepdims=True)
                acc_ref[mp, :, hf * qw:(hf + 1) * qw] = (a * acc_ref[mp, :, hf * qw:(hf + 1) * qw]
                                                         + _dot(vt, pt.astype(BF16)))
                return m_new, l_new

            res = run_chains(j, variant, next_variant, lambda jx: aug_k, finish)
            m_new, l_new = gather(res, 0), gather(res, 1)
            return (m_new[0], l_new[0], m_new[1], l_new[1])

        def below(j, carry):
            return step(j, carry, 0, -c * (i * qb - j * kb).astype(F32), None, 0)

        def above(j, carry):
            return step(j, carry, 1, -c * (j * kb - i * qb).astype(F32), None, 1)

        neg = jnp.full((1, qb), -1e30, F32)
        zero = jnp.zeros((1, qb), F32)
        st_ref[...] = _dot(key_operand(j_lo, 0, aug_k), qt_ref[0, :, 0:qw])
        carry = lax.fori_loop(j_lo, i * ratio, below, (neg, zero, neg, zero))
        for jj in range(ratio):
            carry = step(i * ratio + jj, carry, 0, c * float(jj * kb), jj * kb, 0 if jj + 1 < ratio else 1)
        carry = lax.fori_loop((i + 1) * ratio, j_hi, above, carry)
        l_ref[0] = carry[1]
        l_ref[1] = carry[3]

    pl.when(fixed_ok == 1)(fixed_path)
    pl.when(fixed_ok == 0)(running_path)
    l0, l1 = l_ref[0], l_ref[1]

    dl = dl_ref[...]
    lam = (jnp.exp(jnp.sum(dl[0:1] * dl[1:2], axis=-1, keepdims=True))
           - jnp.exp(jnp.sum(dl[2:3] * dl[3:4], axis=-1, keepdims=True)) + lam_init)
    out = (acc_ref[0] / l0 - lam * (acc_ref[1] / l1)).T
    ms = jnp.mean(out * out, axis=-1, keepdims=True)
    out = out * lax.rsqrt(ms + EPS) * subg_ref[...] * (1.0 - lam_init)
    o_ref[...] = (out * _silu(g_ref[...].astype(F32))).astype(BF16)


def _attention(proj3, diff_lambda, subln_g, *, lam_init, qb, kb):
    b, t, _ = proj3.shape
    hh = ATT_HEADS
    tq = qb
    kern = functools.partial(_attn_kernel, qb=qb, kb=kb, lam_init=lam_init)
    return pl.pallas_call(
        kern,
        out_shape=jax.ShapeDtypeStruct((b, t, hh * ATT_V_DIM), BF16),
        grid=(b, hh, t // tq),
        in_specs=[
            pl.BlockSpec((4, ATT_HEAD_DIM), lambda bi, h, i: (0, 0)),
            pl.BlockSpec((1, ATT_V_DIM), lambda bi, h, i: (0, 0)),
            pl.BlockSpec((None, tq, ATT_V_DIM), lambda bi, h, i: (bi, i, h)),
            pl.BlockSpec((None, t, ATT_V_DIM), lambda bi, h, i: (bi, 0, hh + h)),
            pl.BlockSpec((None, t, ATT_V_DIM), lambda bi, h, i: (bi, 0, 2 * hh + h)),
            pl.BlockSpec((None, tq, ATT_V_DIM), lambda bi, h, i: (bi, i, 3 * hh + h)),
        ],
        out_specs=pl.BlockSpec((None, tq, ATT_V_DIM), lambda bi, h, i: (bi, i, h)),
        scratch_shapes=[
            pltpu.VMEM((2, ATT_V_DIM, qb), F32),
            pltpu.VMEM((4, 2 * ATT_HEAD_DIM, qb), BF16),
            pltpu.VMEM((t // kb, ATT_V_DIM, kb), BF16),
            pltpu.VMEM((kb, min(qb, 2 * LANES)), F32),
            pltpu.VMEM((8, LANES), F32),
            pltpu.VMEM((2, 1, qb), F32),
        ],
        compiler_params=_cparams(("parallel", "parallel", "arbitrary")),
        name="diff_attn",
    )(diff_lambda, subln_g, proj3, proj3, proj3, proj3)


def _conv_kernel(cur_ref, prev_ref, next_ref, w_ref, b_ref, o_ref, u_ref, *, tt):
    i = pl.program_id(1)
    halo = CONV_HALO_ROWS
    pad = (D_CONV - 1) // 2
    prev = prev_ref[...].astype(F32)
    nxt = next_ref[...].astype(F32)
    u_ref[0:halo, :] = jnp.where(i > 0, prev, 0.0)
    u_ref[halo:halo + tt, :] = cur_ref[...].astype(F32)
    u_ref[halo + tt:2 * halo + tt, :] = jnp.where(i < pl.num_programs(1) - 1, nxt, 0.0)
    acc = u_ref[pl.ds(halo - pad, tt), :] * w_ref[0:1, :]
    for k in range(1, D_CONV):
        acc = acc + u_ref[pl.ds(halo - pad + k, tt), :] * w_ref[k:k + 1, :]
    o_ref[...] = _silu(acc + b_ref[...]).astype(BF16)


def _conv(proj3, conv_w, conv_b, *, col0, tt, tc):
    b, t, _ = proj3.shape
    width = conv_w.shape[1]
    halo = CONV_HALO_ROWS
    cb0 = col0 // tc
    rpb = tt // halo
    last = t // halo - 1
    kern = functools.partial(_conv_kernel, tt=tt)
    return pl.pallas_call(
        kern,
        out_shape=jax.ShapeDtypeStruct((b, t, width), BF16),
        grid=(b, t // tt, width // tc),
        in_specs=[
            pl.BlockSpec((None, tt, tc), lambda bi, i, c: (bi, i, cb0 + c)),
            pl.BlockSpec((None, halo, tc), lambda bi, i, c: (bi, jnp.maximum(i * rpb - 1, 0), cb0 + c)),
            pl.BlockSpec((None, halo, tc), lambda bi, i, c: (bi, jnp.minimum((i + 1) * rpb, last), cb0 + c)),
            pl.BlockSpec((D_CONV, tc), lambda bi, i, c: (0, c)),
            pl.BlockSpec((1, tc), lambda bi, i, c: (0, c)),
        ],
        out_specs=pl.BlockSpec((None, tt, tc), lambda bi, i, c: (bi, i, c)),
        scratch_shapes=[pltpu.VMEM((tt + 2 * halo, tc), F32)],
        compiler_params=_cparams(("parallel", "parallel", "parallel")),
        name="conv_silu",
    )(proj3, proj3, proj3, conv_w, conv_b)


def _cumsum_rows(x):
    row = lax.broadcasted_iota(jnp.int32, x.shape, 0)
    sh = 1
    while sh < x.shape[0]:
        x = x + jnp.where(row >= sh, pltpu.roll(x, sh, 0), 0.0)
        sh *= 2
    return x


def _ssd_direction(xa_ref, dtr_ref, dtb_ref, alog_ref, r_ref, s_ref, y_ref, dskip_ref, *, head0, backward):
    L = CHUNK
    d_ssm = SSM_HEADS * SSM_HEAD_DIM
    n_bc = SSM_GROUPS * SSM_STATE
    dt = jax.nn.softplus(dtr_ref[...] + dtb_ref[...])
    a = dt * (-jnp.exp(alog_ref[...]))
    p = _cumsum_rows(a)
    tot = p[L - 1:L, :]
    ac = (tot - p + a) if backward else p
    eo = jnp.exp(ac)
    ds = jnp.exp(tot - ac)
    cd = jnp.broadcast_to(jnp.exp(tot), (8, LANES))
    stack = jnp.concatenate([dt, eo, ds, cd], axis=0)
    ex = _dot(stack.astype(BF16), r_ref[...])
    dtx, eox, dsx, cdx = ex[0:L], ex[L:2 * L], ex[2 * L:3 * L], ex[3 * L:3 * L + 1]

    xs = xa_ref[:, 0:d_ssm].astype(F32)
    xd = xs * dtx
    xdb = xd.astype(BF16)
    wst = (xd * dsx).astype(BF16)
    ac_t = ac.T
    row = lax.broadcasted_iota(jnp.int32, (L, L), 0)
    col = lax.broadcasted_iota(jnp.int32, (L, L), 1)
    mask = (row <= col) if backward else (row >= col)
    lane_lo = col < SSM_HEAD_DIM
    zero_b = jnp.zeros((L, L), BF16)

    for g in range(SSM_GROUPS):
        c0 = g * GROUP_WIDTH
        bg = xa_ref[:, d_ssm + g * SSM_STATE:d_ssm + (g + 1) * SSM_STATE]
        cg = xa_ref[:, d_ssm + n_bc + g * SSM_STATE:d_ssm + n_bc + (g + 1) * SSM_STATE]
        cb = _dot_nt(cg, bg)
        st = s_ref[g]
        y = _dot(cg, st.astype(BF16)) * eox[:, c0:c0 + GROUP_WIDTH]
        if dskip_ref is not None:
            y = y + xs[:, c0:c0 + GROUP_WIDTH] * dskip_ref[:, c0:c0 + GROUP_WIDTH]
        pieces = []
        for pp in range(HEADS_PER_GROUP // 2):
            ms = []
            for e in (2 * pp, 2 * pp + 1):
                hc = head0 + g * HEADS_PER_GROUP + e
                seg = ac[:, hc:hc + 1] - ac_t[hc:hc + 1, :]
                lm = jnp.exp(jnp.where(mask, seg, -jnp.inf))
                ms.append((cb * lm).astype(BF16))
            xpair = xdb[:, c0 + pp * LANES:c0 + (pp + 1) * LANES]
            rhs = jnp.concatenate([jnp.where(lane_lo, xpair, zero_b), jnp.where(lane_lo, zero_b, xpair)], axis=0)
            pieces.append(_dot(jnp.concatenate(ms, axis=1), rhs))
        y = y + jnp.concatenate(pieces, axis=1)
        y_ref[:, c0:c0 + GROUP_WIDTH] = y.astype(BF16)
        bt = bg.astype(F32).T.astype(BF16)
        s_ref[g] = st * cdx[:, c0:c0 + GROUP_WIDTH] + _dot(bt, wst[:, c0:c0 + GROUP_WIDTH])


def _ssd_kernel(dtb_ref, alog_ref, dskip_ref, rf_ref, rb_ref, xaf_ref, dtf_ref, xab_ref, dtbk_ref,
                yf_ref, yb_ref, sf_ref, sb_ref):
    @pl.when(pl.program_id(1) == 0)
    def _():
        sf_ref[...] = jnp.zeros_like(sf_ref)
        sb_ref[...] = jnp.zeros_like(sb_ref)

    _ssd_direction(xaf_ref, dtf_ref, dtb_ref, alog_ref, rf_ref, sf_ref, yf_ref, dskip_ref,
                   head0=0, backward=False)
    _ssd_direction(xab_ref, dtbk_ref, dtb_ref, alog_ref, rb_ref, sb_ref, yb_ref, None,
                   head0=SSM_HEADS, backward=True)


def _ssd(xbc_act, dt_raw3, dtb_row, alog_row, dskip_row, r_f, r_b):
    b, t, wa = xbc_act.shape
    nc = t // CHUNK
    d_ssm = SSM_HEADS * SSM_HEAD_DIM
    const = lambda bi, c: (0, 0)
    return pl.pallas_call(
        _ssd_kernel,
        out_shape=(jax.ShapeDtypeStruct((b, t, d_ssm), BF16), jax.ShapeDtypeStruct((b, t, d_ssm), BF16)),
        grid=(b, nc),
        in_specs=[
            pl.BlockSpec((1, LANES), const),
            pl.BlockSpec((1, LANES), const),
            pl.BlockSpec((1, d_ssm), const),
            pl.BlockSpec((LANES, d_ssm), const),
            pl.BlockSpec((LANES, d_ssm), const),
            pl.BlockSpec((None, CHUNK, wa), lambda bi, c: (bi, c, 0)),
            pl.BlockSpec((None, CHUNK, LANES), lambda bi, c: (bi, c, 0)),
            pl.BlockSpec((None, CHUNK, wa), lambda bi, c: (bi, nc - 1 - c, 0)),
            pl.BlockSpec((None, CHUNK, LANES), lambda bi, c: (bi, nc - 1 - c, 0)),
        ],
        out_specs=(
            pl.BlockSpec((None, CHUNK, d_ssm), lambda bi, c: (bi, c, 0)),
            pl.BlockSpec((None, CHUNK, d_ssm), lambda bi, c: (bi, nc - 1 - c, 0)),
        ),
        scratch_shapes=[
            pltpu.VMEM((SSM_GROUPS, SSM_STATE, GROUP_WIDTH), F32),
            pltpu.VMEM((SSM_GROUPS, SSM_STATE, GROUP_WIDTH), F32),
        ],
        compiler_params=_cparams(("parallel", "arbitrary")),
        name="ssd_scan",
    )(dtb_row, alog_row, dskip_row, r_f, r_b, xbc_act, dt_raw3, xbc_act, dt_raw3)


def _mem_kv_kernel(mem_ref, w_ref, o_ref):
    o_ref[...] = _dot(mem_ref[...].astype(BF16), w_ref[...]).astype(BF16)


def _mem_kv(mem, w_kv):
    b, mt, d = mem.shape
    n = w_kv.shape[1]
    return pl.pallas_call(
        _mem_kv_kernel,
        out_shape=jax.ShapeDtypeStruct((b, mt, n), BF16),
        grid=(b,),
        in_specs=[
            pl.BlockSpec((None, mt, d), lambda bi: (bi, 0, 0)),
            pl.BlockSpec((d, n), lambda bi: (0, 0)),
        ],
        out_specs=pl.BlockSpec((None, mt, n), lambda bi: (bi, 0, 0)),
        compiler_params=_cparams(("parallel",)),
        name="mem_kv_proj",
    )(mem, w_kv)


def _mem_attn_kernel(q_ref, g_ref, kv_ref, o_ref):
    d = MEM_HEAD_DIM
    d_mem = MEM_HEADS * d
    scale = 1.0 / math.sqrt(d)
    for hh in range(MEM_HEADS):
        q = q_ref[:, hh * d:(hh + 1) * d]
        k = kv_ref[:, hh * d:(hh + 1) * d]
        v = kv_ref[:, d_mem + hh * d:d_mem + (hh + 1) * d]
        s = _dot_nt(q, k) * scale
        p = jnp.exp(s - jnp.max(s, axis=-1, keepdims=True))
        o = _dot(p.astype(BF16), v) / jnp.sum(p, axis=-1, keepdims=True)
        o_ref[:, hh * d:(hh + 1) * d] = (o * _silu(g_ref[:, hh * d:(hh + 1) * d].astype(F32))).astype(BF16)


def _mem_attn(proj3, kv, *, q_col0, tq):
    b, t, _ = proj3.shape
    mt, n = kv.shape[1:]
    d_mem = MEM_HEADS * MEM_HEAD_DIM
    qb = q_col0 // d_mem
    return pl.pallas_call(
        _mem_attn_kernel,
        out_shape=jax.ShapeDtypeStruct((b, t, d_mem), BF16),
        grid=(b, t // tq),
        in_specs=[
            pl.BlockSpec((None, tq, d_mem), lambda bi, i: (bi, i, qb)),
            pl.BlockSpec((None, tq, d_mem), lambda bi, i: (bi, i, qb + 1)),
            pl.BlockSpec((None, mt, n), lambda bi, i: (bi, 0, 0)),
        ],
        out_specs=pl.BlockSpec((None, tq, d_mem), lambda bi, i: (bi, i, 0)),
        compiler_params=_cparams(("parallel", "parallel")),
        name="mem_attn",
    )(proj3, proj3, kv)


def _out_proj_kernel(x_ref, gi_ref, bi_ref, ha_ref, yf_ref, yb_ref, z_ref, ng_ref, hm_ref, w_ref, g_ref, b_ref,
                     o_ref):
    da = ha_ref.shape[1]
    dsm = yf_ref.shape[1]
    out = _dot(ha_ref[...], w_ref[0:da, :])
    out = out + _dot(hm_ref[...], w_ref[da + dsm:, :])
    for g in range(SSM_GROUPS):
        c0 = g * GROUP_WIDTH
        yg = ((yf_ref[:, c0:c0 + GROUP_WIDTH].astype(F32) + yb_ref[:, c0:c0 + GROUP_WIDTH].astype(F32))
              * _silu(z_ref[:, c0:c0 + GROUP_WIDTH].astype(F32)))
        ms = jnp.mean(yg * yg, axis=-1, keepdims=True)
        hg = (yg * lax.rsqrt(ms + EPS) * ng_ref[:, c0:c0 + GROUP_WIDTH]).astype(BF16)
        out = out + _dot(hg, w_ref[da + c0:da + c0 + GROUP_WIDTH, :])
    xn = _layer_norm(x_ref[...], gi_ref[...], bi_ref[...])
    o_ref[...] = _layer_norm(ALPHA * xn + out, g_ref[...], b_ref[...])


def _out_proj(x2, ln_in_g, ln_in_b, h_att, y_f, y_b, proj2, norm_g, h_mem, w_out, ln_g, ln_b, *, z_col0, tm):
    m, d = x2.shape
    da, dsm, dm = h_att.shape[1], y_f.shape[1], h_mem.shape[1]
    zb = z_col0 // dsm
    row = lambda i: (i, 0)
    const = lambda i: (0, 0)
    return pl.pallas_call(
        _out_proj_kernel,
        out_shape=jax.ShapeDtypeStruct((m, d), F32),
        grid=(m // tm,),
        in_specs=[
            pl.BlockSpec((tm, d), row),
            pl.BlockSpec((1, d), const),
            pl.BlockSpec((1, d), const),
            pl.BlockSpec((tm, da), row),
            pl.BlockSpec((tm, dsm), row),
            pl.BlockSpec((tm, dsm), row),
            pl.BlockSpec((tm, dsm), lambda i: (i, zb)),
            pl.BlockSpec((1, dsm), const),
            pl.BlockSpec((tm, dm), row),
            pl.BlockSpec(w_out.shape, const, pipeline_mode=pl.Buffered(1)),
            pl.BlockSpec((1, d), const),
            pl.BlockSpec((1, d), const),
        ],
        out_specs=pl.BlockSpec((tm, d), row),
        compiler_params=_cparams(("parallel",)),
        name="out_proj_ln",
    )(x2, ln_in_g, ln_in_b, h_att, y_f, y_b, proj2, norm_g, h_mem, w_out, ln_g, ln_b)


def _tile(n, pref):
    t = min(n, pref)
    while n % t:
        t //= 2
    return t


def _trunk(x, mem, p):
    b, t, d = x.shape
    m = b * t
    x2 = x.reshape(m, d)
    proj2, dt_raw = _in_proj(x2, p["ln_in_g"], p["ln_in_b"], p["w_main"], p["w_dt"],
                             tm=_tile(m, 1024), tn=512)
    n = proj2.shape[1]
    proj3 = proj2.reshape(b, t, n)
    d_att = ATT_HEADS * ATT_V_DIM
    d_ssm = SSM_HEADS * SSM_HEAD_DIM

    h_att = _attention(proj3, p["diff_lambda"], p["subln_g"], lam_init=p["lam_init"],
                       qb=_tile(t, 2048), kb=_tile(t, 512))

    xbc_act = _conv(proj3, p["conv_w"], p["conv_b"], col0=4 * d_att + d_ssm, tt=_tile(t, 512), tc=1024)
    y_f, y_b = _ssd(xbc_act, dt_raw.reshape(b, t, LANES), p["dtb_row"], p["alog_row"], p["dskip_row"],
                    p["r_f"], p["r_b"])

    kv = _mem_kv(mem, p["w_mem_kv"])
    h_mem = _mem_attn(proj3, kv, q_col0=4 * d_att + d_ssm + p["conv_w"].shape[1], tq=_tile(t, 512))

    y = _out_proj(x2, p["ln_in_g"], p["ln_in_b"], h_att.reshape(m, d_att), y_f.reshape(m, d_ssm),
                  y_b.reshape(m, d_ssm), proj2, p["ssm_norm_g"], h_mem.reshape(m, MEM_HEADS * MEM_HEAD_DIM),
                  p["w_out"], p["ln_g"], p["ln_b"], z_col0=4 * d_att, tm=_tile(m, 256))
    return y.reshape(b, t, d)


def _expansion_matrix(head0):
    rows = lax.broadcasted_iota(jnp.int32, (LANES, SSM_HEADS * SSM_HEAD_DIM), 0)
    cols = lax.broadcasted_iota(jnp.int32, (LANES, SSM_HEADS * SSM_HEAD_DIM), 1)
    return (rows == head0 + cols // SSM_HEAD_DIM).astype(BF16)


def kernel(x_prompt, x_sample, mem_prompt, mem_sample, ln_in_g, ln_in_b, w_in, conv_w, conv_b, dt_bias, a_log, d_skip, ssm_norm_g, diff_lambda, subln_g, w_mem_kv, w_out, ln_g, ln_b):
    assert w_in.shape[0] == DEPTH
    d_att = ATT_HEADS * ATT_V_DIM
    d_ssm = SSM_HEADS * SSM_HEAD_DIM
    d_xbc = conv_w.shape[-1]
    dt0 = 4 * d_att + d_ssm + d_xbc
    dt1 = dt0 + 2 * SSM_HEADS
    row = lambda v: v.reshape(1, -1).astype(F32)
    pad_heads = lambda v: jnp.pad(v.reshape(1, -1).astype(F32), ((0, 0), (0, LANES - 2 * SSM_HEADS)))

    l = 0
    w = w_in[l]
    p = {
        "ln_in_g": row(ln_in_g), "ln_in_b": row(ln_in_b),
        "w_main": jnp.concatenate([w[:, :dt0], w[:, dt1:]], axis=1).astype(BF16),
        "w_dt": jnp.pad(w[:, dt0:dt1], ((0, 0), (0, LANES - 2 * SSM_HEADS))).astype(BF16),
        "conv_w": conv_w[l].astype(F32), "conv_b": row(conv_b[l]),
        "dtb_row": pad_heads(dt_bias[l]), "alog_row": pad_heads(a_log[l]),
        "dskip_row": row(jnp.repeat(d_skip[l], SSM_HEAD_DIM)),
        "r_f": _expansion_matrix(0), "r_b": _expansion_matrix(SSM_HEADS),
        "ssm_norm_g": row(ssm_norm_g[l]),
        "diff_lambda": diff_lambda[l].astype(F32), "subln_g": row(subln_g[l]),
        "lam_init": 0.8 - 0.6 * math.exp(-0.3 * l),
        "w_mem_kv": w_mem_kv[l].astype(BF16), "w_out": w_out[l].astype(BF16),
        "ln_g": row(ln_g[l]), "ln_b": row(ln_b[l]),
    }
    return (_trunk(x_prompt, mem_prompt, p), _trunk(x_sample, mem_sample, p))
```

```python
import functools
import math

import jax
import jax.numpy as jnp
from jax import lax
from jax.experimental import pallas as pl
from jax.experimental.pallas import tpu as pltpu

F32 = jnp.float32
BF16 = jnp.bfloat16

DEPTH = 1
EPS = 1e-5
ATT_HEADS = 8
ATT_HEAD_DIM = 128
ATT_V_DIM = 2 * ATT_HEAD_DIM
SSM_HEADS = 32
SSM_HEAD_DIM = 64
SSM_GROUPS = 4
SSM_STATE = 128
HEADS_PER_GROUP = SSM_HEADS // SSM_GROUPS
GROUP_WIDTH = HEADS_PER_GROUP * SSM_HEAD_DIM
D_CONV = 5
CHUNK = 128
MEM_HEADS = 4
MEM_HEAD_DIM = 128
LANES = 128
LOG2E = 1.4426950408889634
UNDERFLOW_LOG2 = 130.0
NORM_BOUND_SLACK = 1.01
FIXED_MAX_RANGE = 100.0
CONV_HALO_ROWS = 16
LN_PIECE_ROWS = 48
ALPHA = (2.0 * DEPTH) ** 0.25
VMEM_LIMIT_BYTES = 56 * 1024 * 1024


def _cparams(sem):
    return pltpu.CompilerParams(dimension_semantics=sem, vmem_limit_bytes=VMEM_LIMIT_BYTES)


def _layer_norm(x, g, b):
    mu = jnp.mean(x, axis=-1, keepdims=True)
    xc = x - mu
    var = jnp.mean(xc * xc, axis=-1, keepdims=True)
    return xc * lax.rsqrt(var + EPS) * g + b


def _silu(x):
    return x * jax.nn.sigmoid(x)


def _dot(a, b):
    return jnp.dot(a, b, preferred_element_type=F32)


def _dot_nt(a, b):
    return lax.dot_general(a, b, (((1,), (1,)), ((), ())), preferred_element_type=F32)


def _in_proj_kernel(x_ref, xnext_ref, g_ref, b_ref, w_ref, wdt_ref, proj_ref, dt_ref, xn_ref):
    i = pl.program_id(0)
    j = pl.program_id(1)
    tm = x_ref.shape[0]
    slot = i % 2

    @pl.when((i == 0) & (j == 0))
    def _():
        xn_ref[0] = _layer_norm(x_ref[...], g_ref[...], b_ref[...]).astype(BF16)

    @pl.when(j == 0)
    def _():
        dt_ref[...] = _dot(xn_ref[slot], wdt_ref[...])

    r0 = pl.multiple_of(jnp.minimum(j * LN_PIECE_ROWS, tm - LN_PIECE_ROWS), 16)
    rows = pl.ds(r0, LN_PIECE_ROWS)
    proj_ref[...] = _dot(xn_ref[slot], w_ref[...]).astype(BF16)
    xn_ref[1 - slot, rows, :] = _layer_norm(xnext_ref[rows, :], g_ref[...], b_ref[...]).astype(BF16)


def _in_proj(x2, ln_g, ln_b, w_main, w_dt, *, tm, tn):
    m, d = x2.shape
    n = w_main.shape[1]
    nrow = m // tm
    assert (n // tn) * LN_PIECE_ROWS >= tm and tm % 16 == 0 and LN_PIECE_ROWS % 16 == 0
    return pl.pallas_call(
        _in_proj_kernel,
        out_shape=(jax.ShapeDtypeStruct((m, n), BF16), jax.ShapeDtypeStruct((m, LANES), F32)),
        grid=(nrow, n // tn),
        in_specs=[
            pl.BlockSpec((tm, d), lambda i, j: (i, 0)),
            pl.BlockSpec((tm, d), lambda i, j: (jnp.minimum(i + 1, nrow - 1), 0)),
            pl.BlockSpec((1, d), lambda i, j: (0, 0)),
            pl.BlockSpec((1, d), lambda i, j: (0, 0)),
            pl.BlockSpec((d, tn), lambda i, j: (0, j)),
            pl.BlockSpec((d, LANES), lambda i, j: (0, 0)),
        ],
        out_specs=(
            pl.BlockSpec((tm, tn), lambda i, j: (i, j)),
            pl.BlockSpec((tm, LANES), lambda i, j: (i, 0)),
        ),
        scratch_shapes=[pltpu.VMEM((2, tm, d), BF16)],
        compiler_params=_cparams(("arbitrary", "arbitrary")),
        name="in_proj",
    )(x2, x2, ln_g, ln_b, w_main, w_dt)


def _attn_kernel(dl_ref, subg_ref, q_ref, k_ref, v_ref, g_ref, o_ref, acc_ref, qt_ref, vt_ref, st_ref, kn_ref, l_ref,
                 *, qb, kb, lam_init):
    h = pl.program_id(1)
    i = pl.program_id(2)
    nblk = k_ref.shape[0] // kb
    ratio = qb // kb
    d = ATT_HEAD_DIM
    c = jnp.exp2(-(h + 1).astype(F32)) * LOG2E

    @pl.when(i == 0)
    def _():
        def tr(jj, kn2):
            start = pl.multiple_of(jj * kb, kb)
            vt_ref[jj] = v_ref[pl.ds(start, kb), :].astype(F32).T.astype(BF16)
            kf = k_ref[pl.ds(start, kb), :].astype(F32)
            sq = kf * kf
            for mp in range(2):
                rows = jnp.sum(sq[:, mp * d:(mp + 1) * d], axis=1, keepdims=True)
                kn2 = jnp.maximum(kn2, jnp.max(rows, axis=0, keepdims=True))
            return kn2
        kn2 = lax.fori_loop(0, nblk, tr, jnp.zeros((1, 1), F32))
        kn_ref[...] = jnp.broadcast_to(kn2, kn_ref.shape)

    def split3(x):
        hi = x.astype(BF16).astype(F32)
        mid = (x - hi).astype(BF16).astype(F32)
        return hi, mid, x - hi - mid

    q = q_ref[...].astype(F32) * (LOG2E / math.sqrt(d))
    qt, qrow2 = [], []
    for mp in range(2):
        qtf = q[:, mp * d:(mp + 1) * d].T
        qrow2.append(jnp.sum(qtf * qtf, axis=0, keepdims=True))
        qt.append(qtf.astype(BF16))
    qn2 = jnp.max(jnp.maximum(qrow2[0], qrow2[1]), axis=1, keepdims=True)
    kmax = jnp.sqrt(kn_ref[0:1, 0:1]) * NORM_BOUND_SLACK

    bound = jnp.sqrt(qn2) * kmax
    reach = (2.0 * bound + UNDERFLOW_LOG2) / c
    q0 = (i * qb).astype(F32)
    seq_last = float(k_ref.shape[0] - 1)
    first = jnp.floor(jnp.maximum(q0 - reach, 0.0) * (1.0 / kb)).astype(jnp.int32)[0, 0]
    last = jnp.floor(jnp.minimum(q0 + (qb - 1) + reach, seq_last) * (1.0 / kb)).astype(jnp.int32)[0, 0]
    j_lo = jnp.minimum(first, i * ratio)
    j_hi = jnp.maximum(last + 1, (i + 1) * ratio)
    fixed_ok = (2.0 * bound <= FIXED_MAX_RANGE).astype(jnp.int32)[0, 0]

    acc_ref[...] = jnp.zeros_like(acc_ref)
    qw = st_ref.shape[1]
    nq = qb // qw
    chains = [(mp, hf) for mp in range(2) for hf in range(nq)]
    row = lax.broadcasted_iota(jnp.int32, (d, qb), 0)
    lane = lax.broadcasted_iota(jnp.int32, (kb, d), 1)
    cr = c * lax.broadcasted_iota(jnp.int32, (d, qb), 1).astype(F32)
    upos = lax.broadcasted_iota(jnp.int32, (kb, d), 0).astype(F32)

    def query_aug(x, ones):
        x = split3(x)
        return jnp.where(row == 0, x[0], jnp.where(row == 1, x[1], jnp.where(row == 2, x[2],
                         jnp.where(row < 6, ones, 0.0)))).astype(BF16)

    def key_aug(y):
        y = split3(y)
        return jnp.where(lane < 3, -1.0, jnp.where(lane == 3, y[0], jnp.where(lane == 4, y[1],
                         jnp.where(lane == 5, y[2], 0.0)))).astype(BF16)

    def key_operand(j, mp, aug):
        start = pl.multiple_of(j * kb, kb)
        return jnp.concatenate([k_ref[pl.ds(start, kb), mp * d:(mp + 1) * d], aug], axis=1)

    def overlap_term(overlap, hf):
        ahead = (lax.broadcasted_iota(jnp.int32, (kb, qw), 0) - lax.broadcasted_iota(jnp.int32, (kb, qw), 1)
                 + (overlap - hf * qw))
        return (-2.0 * c) * jnp.maximum(ahead, 0).astype(F32)

    def run_chains(j, variant, next_variant, aug_of, finish):
        kp = [key_operand(j, mp, aug_of(j)) for mp in range(2)]
        results = {}
        pending = st_ref[...]
        for n, (mp, hf) in enumerate(chains):
            st = pending
            if n + 1 < len(chains):
                mp1, hf1 = chains[n + 1]
                pending = _dot(kp[mp1], qt_ref[2 * variant + mp1, :, hf1 * qw:(hf1 + 1) * qw])
            else:
                jn = jnp.minimum(j + 1, nblk - 1)
                st_ref[...] = _dot(key_operand(jn, 0, aug_of(jn)), qt_ref[2 * next_variant, :, 0:qw])
            results[(mp, hf)] = finish(mp, hf, st)
        return results

    def gather(results, idx):
        return [jnp.concatenate([results[(mp, hf)][idx] for hf in range(nq)], axis=1) for mp in range(2)]

    def fixed_path():
        for mp in range(2):
            m_ub = jnp.sqrt(qrow2[mp]) * kmax
            qt_ref[mp] = jnp.concatenate([qt[mp], query_aug(m_ub + cr, 1.0)], axis=0)
            qt_ref[2 + mp] = jnp.concatenate([qt[mp], query_aug(m_ub - cr, -1.0)], axis=0)

        def aug_of(j):
            return key_aug(c * ((j * kb - i * qb).astype(F32) + upos))

        def step(j, carry, variant, overlap, next_variant):
            vt = vt_ref[j]

            def finish(mp, hf, st):
                if overlap is not None:
                    st = st + overlap_term(overlap, hf)
                pt = jnp.exp2(st)
                acc_ref[mp, :, hf * qw:(hf + 1) * qw] += _dot(vt, pt.astype(BF16))
                return (jnp.sum(pt, axis=0, keepdims=True),)

            sums = gather(run_chains(j, variant, next_variant, aug_of, finish), 0)
            return (carry[0] + sums[0], carry[1] + sums[1])

        zero = jnp.zeros((1, qb), F32)
        st_ref[...] = _dot(key_operand(j_lo, 0, aug_of(j_lo)), qt_ref[0, :, 0:qw])
        carry = lax.fori_loop(j_lo, i * ratio, lambda j, cy: step(j, cy, 0, None, 0), (zero, zero))
        for jj in range(ratio):
            carry = step(i * ratio + jj, carry, 0, jj * kb, 0 if jj + 1 < ratio else 1)
        carry = lax.fori_loop((i + 1) * ratio, j_hi, lambda j, cy: step(j, cy, 1, None, 1), carry)
        l_ref[0] = carry[0]
        l_ref[1] = carry[1]

    def running_path():
        aug_k = key_aug(c * upos)
        for mp in range(2):
            qt_ref[mp] = jnp.concatenate([qt[mp], query_aug(cr, 1.0)], axis=0)
            qt_ref[2 + mp] = jnp.concatenate([qt[mp], query_aug(-cr, -1.0)], axis=0)

        def step(j, carry, variant, delta, overlap, next_variant):
            vt = vt_ref[j]

            def finish(mp, hf, st):
                if overlap is not None:
                    st = st + overlap_term(overlap, hf)
                m_old, l_old = carry[2 * mp][:, hf * qw:(hf + 1) * qw], carry[2 * mp + 1][:, hf * qw:(hf + 1) * qw]
                m_new = jnp.maximum(m_old, jnp.max(st, axis=0, keepdims=True) + delta)
                a = jnp.exp2(m_old - m_new)
                pt = jnp.exp2(st - (m_new - delta))
                l_new = a * l_old + jnp.sum(pt, axis=0, keepdims=True)
                acc_ref[mp, :, hf * qw:(hf + 1) * qw] = (a * acc_ref[mp, :, hf * qw:(hf + 1) * qw]
                                                         + _dot(vt, pt.astype(BF16)))
                return m_new, l_new

            res = run_chains(j, variant, next_variant, lambda jx: aug_k, finish)
            m_new, l_new = gather(res, 0), gather(res, 1)
            return (m_new[0], l_new[0], m_new[1], l_new[1])

        def below(j, carry):
            return step(j, carry, 0, -c * (i * qb - j * kb).astype(F32), None, 0)

        def above(j, carry):
            return step(j, carry, 1, -c * (j * kb - i * qb).astype(F32), None, 1)

        neg = jnp.full((1, qb), -1e30, F32)
        zero = jnp.zeros((1, qb), F32)
        st_ref[...] = _dot(key_operand(j_lo, 0, aug_k), qt_ref[0, :, 0:qw])
        carry = lax.fori_loop(j_lo, i * ratio, below, (neg, zero, neg, zero))
        for jj in range(ratio):
            carry = step(i * ratio + jj, carry, 0, c * float(jj * kb), jj * kb, 0 if jj + 1 < ratio else 1)
        carry = lax.fori_loop((i + 1) * ratio, j_hi, above, carry)
        l_ref[0] = carry[1]
        l_ref[1] = carry[3]

    pl.when(fixed_ok == 1)(fixed_path)
    pl.when(fixed_ok == 0)(running_path)
    l0, l1 = l_ref[0], l_ref[1]

    dl = dl_ref[...]
    lam = (jnp.exp(jnp.sum(dl[0:1] * dl[1:2], axis=-1, keepdims=True))
           - jnp.exp(jnp.sum(dl[2:3] * dl[3:4], axis=-1, keepdims=True)) + lam_init)
    out = (acc_ref[0] / l0 - lam * (acc_ref[1] / l1)).T
    ms = jnp.mean(out * out, axis=-1, keepdims=True)
    out = out * lax.rsqrt(ms + EPS) * subg_ref[...] * (1.0 - lam_init)
    o_ref[...] = (out * _silu(g_ref[...].astype(F32))).astype(BF16)


def _attention(proj3, diff_lambda, subln_g, *, lam_init, qb, kb):
    b, t, _ = proj3.shape
    hh = ATT_HEADS
    tq = qb
    kern = functools.partial(_attn_kernel, qb=qb, kb=kb, lam_init=lam_init)
    return pl.pallas_call(
        kern,
        out_shape=jax.ShapeDtypeStruct((b, t, hh * ATT_V_DIM), BF16),
        grid=(b, hh, t // tq),
        in_specs=[
            pl.BlockSpec((4, ATT_HEAD_DIM), lambda bi, h, i: (0, 0)),
            pl.BlockSpec((1, ATT_V_DIM), lambda bi, h, i: (0, 0)),
            pl.BlockSpec((None, tq, ATT_V_DIM), lambda bi, h, i: (bi, i, h)),
            pl.BlockSpec((None, t, ATT_V_DIM), lambda bi, h, i: (bi, 0, hh + h)),
            pl.BlockSpec((None, t, ATT_V_DIM), lambda bi, h, i: (bi, 0, 2 * hh + h)),
            pl.BlockSpec((None, tq, ATT_V_DIM), lambda bi, h, i: (bi, i, 3 * hh + h)),
        ],
        out_specs=pl.BlockSpec((None, tq, ATT_V_DIM), lambda bi, h, i: (bi, i, h)),
        scratch_shapes=[
            pltpu.VMEM((2, ATT_V_DIM, qb), F32),
            pltpu.VMEM((4, 2 * ATT_HEAD_DIM, qb), BF16),
            pltpu.VMEM((t // kb, ATT_V_DIM, kb), BF16),
            pltpu.VMEM((kb, min(qb, 2 * LANES)), F32),
            pltpu.VMEM((8, LANES), F32),
            pltpu.VMEM((2, 1, qb), F32),
        ],
        compiler_params=_cparams(("parallel", "parallel", "arbitrary")),
        name="diff_attn",
    )(diff_lambda, subln_g, proj3, proj3, proj3, proj3)


def _conv_kernel(cur_ref, prev_ref, next_ref, w_ref, b_ref, o_ref, u_ref, *, tt):
    i = pl.program_id(1)
    halo = CONV_HALO_ROWS
    pad = (D_CONV - 1) // 2
    prev = prev_ref[...].astype(F32)
    nxt = next_ref[...].astype(F32)
    u_ref[0:halo, :] = jnp.where(i > 0, prev, 0.0)
    u_ref[halo:halo + tt, :] = cur_ref[...].astype(F32)
    u_ref[halo + tt:2 * halo + tt, :] = jnp.where(i < pl.num_programs(1) - 1, nxt, 0.0)
    u = u_ref[...]
    n = tt + 2 * halo
    down = lambda x: pltpu.roll(x, 1, 0)
    up = lambda x: pltpu.roll(x, n - 1, 0)
    before = u * w_ref[0:1, :]
    for k in range(1, pad):
        before = down(before) + u * w_ref[k:k + 1, :]
    after = u * w_ref[D_CONV - 1:D_CONV, :]
    for k in range(D_CONV - 2, pad, -1):
        after = up(after) + u * w_ref[k:k + 1, :]
    acc = u * w_ref[pad:pad + 1, :] + down(before) + up(after)
    o_ref[...] = _silu(acc[halo:halo + tt] + b_ref[...]).astype(BF16)


def _conv(proj3, conv_w, conv_b, *, col0, tt, tc):
    b, t, _ = proj3.shape
    width = conv_w.shape[1]
    halo = CONV_HALO_ROWS
    cb0 = col0 // tc
    rpb = tt // halo
    last = t // halo - 1
    kern = functools.partial(_conv_kernel, tt=tt)
    return pl.pallas_call(
        kern,
        out_shape=jax.ShapeDtypeStruct((b, t, width), BF16),
        grid=(b, t // tt, width // tc),
        in_specs=[
            pl.BlockSpec((None, tt, tc), lambda bi, i, c: (bi, i, cb0 + c)),
            pl.BlockSpec((None, halo, tc), lambda bi, i, c: (bi, jnp.maximum(i * rpb - 1, 0), cb0 + c)),
            pl.BlockSpec((None, halo, tc), lambda bi, i, c: (bi, jnp.minimum((i + 1) * rpb, last), cb0 + c)),
            pl.BlockSpec((D_CONV, tc), lambda bi, i, c: (0, c)),
            pl.BlockSpec((1, tc), lambda bi, i, c: (0, c)),
        ],
        out_specs=pl.BlockSpec((None, tt, tc), lambda bi, i, c: (bi, i, c)),
        scratch_shapes=[pltpu.VMEM((tt + 2 * halo, tc), F32)],
        compiler_params=_cparams(("parallel", "parallel", "parallel")),
        name="conv_silu",
    )(proj3, proj3, proj3, conv_w, conv_b)


def _cumsum_rows(x):
    row = lax.broadcasted_iota(jnp.int32, x.shape, 0)
    sh = 1
    while sh < x.shape[0]:
        x = x + jnp.where(row >= sh, pltpu.roll(x, sh, 0), 0.0)
        sh *= 2
    return x


def _ssd_direction(xa_ref, dtr_ref, dtb_ref, alog_ref, r_ref, s_ref, y_ref, dskip_ref, *, head0, backward):
    L = CHUNK
    d_ssm = SSM_HEADS * SSM_HEAD_DIM
    n_bc = SSM_GROUPS * SSM_STATE
    dt = jax.nn.softplus(dtr_ref[...] + dtb_ref[...])
    a = dt * (-jnp.exp(alog_ref[...]))
    p = _cumsum_rows(a)
    tot = p[L - 1:L, :]
    ac = (tot - p + a) if backward else p
    eo = jnp.exp(ac)
    ds = jnp.exp(tot - ac)
    cd = jnp.broadcast_to(jnp.exp(tot), (8, LANES))
    stack = jnp.concatenate([dt, eo, ds, cd], axis=0)
    ex = _dot(stack.astype(BF16), r_ref[...])
    dtx, eox, dsx, cdx = ex[0:L], ex[L:2 * L], ex[2 * L:3 * L], ex[3 * L:3 * L + 1]

    xs = xa_ref[:, 0:d_ssm].astype(F32)
    xd = xs * dtx
    xdb = xd.astype(BF16)
    wst = (xd * dsx).astype(BF16)
    ac_t = ac.T
    row = lax.broadcasted_iota(jnp.int32, (L, L), 0)
    col = lax.broadcasted_iota(jnp.int32, (L, L), 1)
    mask = (row <= col) if backward else (row >= col)
    lane_lo = col < SSM_HEAD_DIM
    zero_b = jnp.zeros((L, L), BF16)

    for g in range(SSM_GROUPS):
        c0 = g * GROUP_WIDTH
        bg = xa_ref[:, d_ssm + g * SSM_STATE:d_ssm + (g + 1) * SSM_STATE]
        cg = xa_ref[:, d_ssm + n_bc + g * SSM_STATE:d_ssm + n_bc + (g + 1) * SSM_STATE]
        cb = _dot_nt(cg, bg)
        st = s_ref[g]
        y = _dot(cg, st.astype(BF16)) * eox[:, c0:c0 + GROUP_WIDTH]
        if dskip_ref is not None:
            y = y + xs[:, c0:c0 + GROUP_WIDTH] * dskip_ref[:, c0:c0 + GROUP_WIDTH]
        pieces = []
        for pp in range(HEADS_PER_GROUP // 2):
            ms = []
            for e in (2 * pp, 2 * pp + 1):
                hc = head0 + g * HEADS_PER_GROUP + e
                seg = ac[:, hc:hc + 1] - ac_t[hc:hc + 1, :]
                lm = jnp.exp(jnp.where(mask, seg, -jnp.inf))
                ms.append((cb * lm).astype(BF16))
            xpair = xdb[:, c0 + pp * LANES:c0 + (pp + 1) * LANES]
            rhs = jnp.concatenate([jnp.where(lane_lo, xpair, zero_b), jnp.where(lane_lo, zero_b, xpair)], axis=0)
            pieces.append(_dot(jnp.concatenate(ms, axis=1), rhs))
        y = y + jnp.concatenate(pieces, axis=1)
        y_ref[:, c0:c0 + GROUP_WIDTH] = y.astype(BF16)
        bt = bg.astype(F32).T.astype(BF16)
        s_ref[g] = st * cdx[:, c0:c0 + GROUP_WIDTH] + _dot(bt, wst[:, c0:c0 + GROUP_WIDTH])


def _ssd_kernel(dtb_ref, alog_ref, dskip_ref, rf_ref, rb_ref, xaf_ref, dtf_ref, xab_ref, dtbk_ref,
                yf_ref, yb_ref, sf_ref, sb_ref):
    @pl.when(pl.program_id(1) == 0)
    def _():
        sf_ref[...] = jnp.zeros_like(sf_ref)
        sb_ref[...] = jnp.zeros_like(sb_ref)

    _ssd_direction(xaf_ref, dtf_ref, dtb_ref, alog_ref, rf_ref, sf_ref, yf_ref, dskip_ref,
                   head0=0, backward=False)
    _ssd_direction(xab_ref, dtbk_ref, dtb_ref, alog_ref, rb_ref, sb_ref, yb_ref, None,
                   head0=SSM_HEADS, backward=True)


def _ssd(xbc_act, dt_raw3, dtb_row, alog_row, dskip_row, r_f, r_b):
    b, t, wa = xbc_act.shape
    nc = t // CHUNK
    d_ssm = SSM_HEADS * SSM_HEAD_DIM
    const = lambda bi, c: (0, 0)
    return pl.pallas_call(
        _ssd_kernel,
        out_shape=(jax.ShapeDtypeStruct((b, t, d_ssm), BF16), jax.ShapeDtypeStruct((b, t, d_ssm), BF16)),
        grid=(b, nc),
        in_specs=[
            pl.BlockSpec((1, LANES), const),
            pl.BlockSpec((1, LANES), const),
            pl.BlockSpec((1, d_ssm), const),
            pl.BlockSpec((LANES, d_ssm), const),
            pl.BlockSpec((LANES, d_ssm), const),
            pl.BlockSpec((None, CHUNK, wa), lambda bi, c: (bi, c, 0)),
            pl.BlockSpec((None, CHUNK, LANES), lambda bi, c: (bi, c, 0)),
            pl.BlockSpec((None, CHUNK, wa), lambda bi, c: (bi, nc - 1 - c, 0)),
            pl.BlockSpec((None, CHUNK, LANES), lambda bi, c: (bi, nc - 1 - c, 0)),
        ],
        out_specs=(
            pl.BlockSpec((None, CHUNK, d_ssm), lambda bi, c: (bi, c, 0)),
            pl.BlockSpec((None, CHUNK, d_ssm), lambda bi, c: (bi, nc - 1 - c, 0)),
        ),
        scratch_shapes=[
            pltpu.VMEM((SSM_GROUPS, SSM_STATE, GROUP_WIDTH), F32),
            pltpu.VMEM((SSM_GROUPS, SSM_STATE, GROUP_WIDTH), F32),
        ],
        compiler_params=_cparams(("parallel", "arbitrary")),
        name="ssd_scan",
    )(dtb_row, alog_row, dskip_row, r_f, r_b, xbc_act, dt_raw3, xbc_act, dt_raw3)


def _mem_kv_kernel(mem_ref, w_ref, o_ref):
    o_ref[...] = _dot(mem_ref[...].astype(BF16), w_ref[...]).astype(BF16)


def _mem_kv(mem, w_kv):
    b, mt, d = mem.shape
    n = w_kv.shape[1]
    return pl.pallas_call(
        _mem_kv_kernel,
        out_shape=jax.ShapeDtypeStruct((b, mt, n), BF16),
        grid=(b,),
        in_specs=[
            pl.BlockSpec((None, mt, d), lambda bi: (bi, 0, 0)),
            pl.BlockSpec((d, n), lambda bi: (0, 0)),
        ],
        out_specs=pl.BlockSpec((None, mt, n), lambda bi: (bi, 0, 0)),
        compiler_params=_cparams(("parallel",)),
        name="mem_kv_proj",
    )(mem, w_kv)


def _mem_attn_kernel(q_ref, g_ref, kv_ref, o_ref):
    d = MEM_HEAD_DIM
    d_mem = MEM_HEADS * d
    scale = 1.0 / math.sqrt(d)
    for hh in range(MEM_HEADS):
        q = q_ref[:, hh * d:(hh + 1) * d]
        k = kv_ref[:, hh * d:(hh + 1) * d]
        v = kv_ref[:, d_mem + hh * d:d_mem + (hh + 1) * d]
        s = _dot_nt(q, k) * scale
        p = jnp.exp(s - jnp.max(s, axis=-1, keepdims=True))
        o = _dot(p.astype(BF16), v) / jnp.sum(p, axis=-1, keepdims=True)
        o_ref[:, hh * d:(hh + 1) * d] = (o * _silu(g_ref[:, hh * d:(hh + 1) * d].astype(F32))).astype(BF16)


def _mem_attn(proj3, kv, *, q_col0, tq):
    b, t, _ = proj3.shape
    mt, n = kv.shape[1:]
    d_mem = MEM_HEADS * MEM_HEAD_DIM
    qb = q_col0 // d_mem
    return pl.pallas_call(
        _mem_attn_kernel,
        out_shape=jax.ShapeDtypeStruct((b, t, d_mem), BF16),
        grid=(b, t // tq),
        in_specs=[
            pl.BlockSpec((None, tq, d_mem), lambda bi, i: (bi, i, qb)),
            pl.BlockSpec((None, tq, d_mem), lambda bi, i: (bi, i, qb + 1)),
            pl.BlockSpec((None, mt, n), lambda bi, i: (bi, 0, 0)),
        ],
        out_specs=pl.BlockSpec((None, tq, d_mem), lambda bi, i: (bi, i, 0)),
        compiler_params=_cparams(("parallel", "parallel")),
        name="mem_attn",
    )(proj3, proj3, kv)


def _out_proj_kernel(x_ref, gi_ref, bi_ref, ha_ref, yf_ref, yb_ref, z_ref, ng_ref, hm_ref, w_ref, g_ref, b_ref,
                     o_ref):
    da = ha_ref.shape[1]
    dsm = yf_ref.shape[1]
    out = _dot(ha_ref[...], w_ref[0:da, :])
    out = out + _dot(hm_ref[...], w_ref[da + dsm:, :])
    for g in range(SSM_GROUPS):
        c0 = g * GROUP_WIDTH
        yg = ((yf_ref[:, c0:c0 + GROUP_WIDTH].astype(F32) + yb_ref[:, c0:c0 + GROUP_WIDTH].astype(F32))
              * _silu(z_ref[:, c0:c0 + GROUP_WIDTH].astype(F32)))
        ms = jnp.mean(yg * yg, axis=-1, keepdims=True)
        hg = (yg * lax.rsqrt(ms + EPS) * ng_ref[:, c0:c0 + GROUP_WIDTH]).astype(BF16)
        out = out + _dot(hg, w_ref[da + c0:da + c0 + GROUP_WIDTH, :])
    xn = _layer_norm(x_ref[...], gi_ref[...], bi_ref[...])
    o_ref[...] = _layer_norm(ALPHA * xn + out, g_ref[...], b_ref[...])


def _out_proj(x2, ln_in_g, ln_in_b, h_att, y_f, y_b, proj2, norm_g, h_mem, w_out, ln_g, ln_b, *, z_col0, tm):
    m, d = x2.shape
    da, dsm, dm = h_att.shape[1], y_f.shape[1], h_mem.shape[1]
    zb = z_col0 // dsm
    row = lambda i: (i, 0)
    const = lambda i: (0, 0)
    return pl.pallas_call(
        _out_proj_kernel,
        out_shape=jax.ShapeDtypeStruct((m, d), F32),
        grid=(m // tm,),
        in_specs=[
            pl.BlockSpec((tm, d), row),
            pl.BlockSpec((1, d), const),
            pl.BlockSpec((1, d), const),
            pl.BlockSpec((tm, da), row),
            pl.BlockSpec((tm, dsm), row),
            pl.BlockSpec((tm, dsm), row),
            pl.BlockSpec((tm, dsm), lambda i: (i, zb)),
            pl.BlockSpec((1, dsm), const),
            pl.BlockSpec((tm, dm), row),
            pl.BlockSpec(w_out.shape, const, pipeline_mode=pl.Buffered(1)),
            pl.BlockSpec((1, d), const),
            pl.BlockSpec((1, d), const),
        ],
        out_specs=pl.BlockSpec((tm, d), row),
        compiler_params=_cparams(("parallel",)),
        name="out_proj_ln",
    )(x2, ln_in_g, ln_in_b, h_att, y_f, y_b, proj2, norm_g, h_mem, w_out, ln_g, ln_b)


def _tile(n, pref):
    t = min(n, pref)
    while n % t:
        t //= 2
    return t


def _trunk(x, mem, p):
    b, t, d = x.shape
    m = b * t
    x2 = x.reshape(m, d)
    proj2, dt_raw = _in_proj(x2, p["ln_in_g"], p["ln_in_b"], p["w_main"], p["w_dt"],
                             tm=_tile(m, 1024), tn=512)
    n = proj2.shape[1]
    proj3 = proj2.reshape(b, t, n)
    d_att = ATT_HEADS * ATT_V_DIM
    d_ssm = SSM_HEADS * SSM_HEAD_DIM

    h_att = _attention(proj3, p["diff_lambda"], p["subln_g"], lam_init=p["lam_init"],
                       qb=_tile(t, 2048), kb=_tile(t, 512))

    xbc_act = _conv(proj3, p["conv_w"], p["conv_b"], col0=4 * d_att + d_ssm, tt=_tile(t, 512), tc=1024)
    y_f, y_b = _ssd(xbc_act, dt_raw.reshape(b, t, LANES), p["dtb_row"], p["alog_row"], p["dskip_row"],
                    p["r_f"], p["r_b"])

    kv = _mem_kv(mem, p["w_mem_kv"])
    h_mem = _mem_attn(proj3, kv, q_col0=4 * d_att + d_ssm + p["conv_w"].shape[1], tq=_tile(t, 512))

    y = _out_proj(x2, p["ln_in_g"], p["ln_in_b"], h_att.reshape(m, d_att), y_f.reshape(m, d_ssm),
                  y_b.reshape(m, d_ssm), proj2, p["ssm_norm_g"], h_mem.reshape(m, MEM_HEADS * MEM_HEAD_DIM),
                  p["w_out"], p["ln_g"], p["ln_b"], z_col0=4 * d_att, tm=_tile(m, 256))
    return y.reshape(b, t, d)


def _expansion_matrix(head0):
    rows = lax.broadcasted_iota(jnp.int32, (LANES, SSM_HEADS * SSM_HEAD_DIM), 0)
    cols = lax.broadcasted_iota(jnp.int32, (LANES, SSM_HEADS * SSM_HEAD_DIM), 1)
    return (rows == head0 + cols // SSM_HEAD_DIM).astype(BF16)


def kernel(x_prompt, x_sample, mem_prompt, mem_sample, ln_in_g, ln_in_b, w_in, conv_w, conv_b, dt_bias, a_log, d_skip, ssm_norm_g, diff_lambda, subln_g, w_mem_kv, w_out, ln_g, ln_b):
    assert w_in.shape[0] == DEPTH
    d_att = ATT_HEADS * ATT_V_DIM
    d_ssm = SSM_HEADS * SSM_HEAD_DIM
    d_xbc = conv_w.shape[-1]
    dt0 = 4 * d_att + d_ssm + d_xbc
    dt1 = dt0 + 2 * SSM_HEADS
    row = lambda v: v.reshape(1, -1).astype(F32)
    pad_heads = lambda v: jnp.pad(v.reshape(1, -1).astype(F32), ((0, 0), (0, LANES - 2 * SSM_HEADS)))

    l = 0
    w = w_in[l]
    p = {
        "ln_in_g": row(ln_in_g), "ln_in_b": row(ln_in_b),
        "w_main": jnp.concatenate([w[:, :dt0], w[:, dt1:]], axis=1).astype(BF16),
        "w_dt": jnp.pad(w[:, dt0:dt1], ((0, 0), (0, LANES - 2 * SSM_HEADS))).astype(BF16),
        "conv_w": conv_w[l].astype(F32), "conv_b": row(conv_b[l]),
        "dtb_row": pad_heads(dt_bias[l]), "alog_row": pad_heads(a_log[l]),
        "dskip_row": row(jnp.repeat(d_skip[l], SSM_HEAD_DIM)),
        "r_f": _expansion_matrix(0), "r_b": _expansion_matrix(SSM_HEADS),
        "ssm_norm_g": row(ssm_norm_g[l]),
        "diff_lambda": diff_lambda[l].astype(F32), "subln_g": row(subln_g[l]),
        "lam_init": 0.8 - 0.6 * math.exp(-0.3 * l),
        "w_mem_kv": w_mem_kv[l].astype(BF16), "w_out": w_out[l].astype(BF16),
        "ln_g": row(ln_g[l]), "ln_b": row(ln_b[l]),
    }
    return (_trunk(x_prompt, mem_prompt, p), _trunk(x_sample, mem_sample, p))
```

```python
import functools
import math

import jax
import jax.numpy as jnp
from jax import lax
from jax.experimental import pallas as pl
from jax.experimental.pallas import tpu as pltpu

F32 = jnp.float32
BF16 = jnp.bfloat16

DEPTH = 1
EPS = 1e-5
ATT_HEADS = 8
ATT_HEAD_DIM = 128
ATT_V_DIM = 2 * ATT_HEAD_DIM
SSM_HEADS = 32
SSM_HEAD_DIM = 64
SSM_GROUPS = 4
SSM_STATE = 128
HEADS_PER_GROUP = SSM_HEADS // SSM_GROUPS
GROUP_WIDTH = HEADS_PER_GROUP * SSM_HEAD_DIM
D_CONV = 5
CHUNK = 128
MEM_HEADS = 4
MEM_HEAD_DIM = 128
LANES = 128
LOG2E = 1.4426950408889634
UNDERFLOW_LOG2 = 130.0
NORM_BOUND_SLACK = 1.01
FIXED_MAX_RANGE = 64.0
CONV_HALO_ROWS = 16
ALPHA = (2.0 * DEPTH) ** 0.25
VMEM_LIMIT_BYTES = 56 * 1024 * 1024


def _cparams(sem):
    return pltpu.CompilerParams(dimension_semantics=sem, vmem_limit_bytes=VMEM_LIMIT_BYTES)


def _layer_norm(x, g, b):
    mu = jnp.mean(x, axis=-1, keepdims=True)
    xc = x - mu
    var = jnp.mean(xc * xc, axis=-1, keepdims=True)
    return xc * lax.rsqrt(var + EPS) * g + b


def _silu(x):
    return x * jax.nn.sigmoid(x)


def _dot(a, b):
    return jnp.dot(a, b, preferred_element_type=F32)


def _dot_nt(a, b):
    return lax.dot_general(a, b, (((1,), (1,)), ((), ())), preferred_element_type=F32)


def _in_proj_kernel(x_ref, g_ref, b_ref, w_ref, wdt_ref, proj_ref, dt_ref, xn_ref):
    @pl.when(pl.program_id(1) == 0)
    def _():
        xn = _layer_norm(x_ref[...], g_ref[...], b_ref[...]).astype(BF16)
        xn_ref[...] = xn
        dt_ref[...] = _dot(xn, wdt_ref[...])

    proj_ref[...] = _dot(xn_ref[...], w_ref[...]).astype(BF16)


def _in_proj(x2, ln_g, ln_b, w_main, w_dt, *, tm, tn):
    m, d = x2.shape
    n = w_main.shape[1]
    return pl.pallas_call(
        _in_proj_kernel,
        out_shape=(jax.ShapeDtypeStruct((m, n), BF16), jax.ShapeDtypeStruct((m, LANES), F32)),
        grid=(m // tm, n // tn),
        in_specs=[
            pl.BlockSpec((tm, d), lambda i, j: (i, 0)),
            pl.BlockSpec((1, d), lambda i, j: (0, 0)),
            pl.BlockSpec((1, d), lambda i, j: (0, 0)),
            pl.BlockSpec((d, tn), lambda i, j: (0, j)),
            pl.BlockSpec((d, LANES), lambda i, j: (0, 0)),
        ],
        out_specs=(
            pl.BlockSpec((tm, tn), lambda i, j: (i, j)),
            pl.BlockSpec((tm, LANES), lambda i, j: (i, 0)),
        ),
        scratch_shapes=[pltpu.VMEM((tm, d), BF16)],
        compiler_params=_cparams(("parallel", "arbitrary")),
        name="in_proj",
    )(x2, ln_g, ln_b, w_main, w_dt)


def _attn_kernel(dl_ref, subg_ref, q_ref, k_ref, v_ref, g_ref, o_ref, acc_ref, qt_ref, vt_ref, st_ref, kn_ref, l_ref,
                 *, qb, kb, lam_init):
    h = pl.program_id(1)
    i = pl.program_id(2)
    nblk = k_ref.shape[0] // kb
    ratio = qb // kb
    d = ATT_HEAD_DIM
    c = jnp.exp2(-(h + 1).astype(F32)) * LOG2E

    @pl.when(i == 0)
    def _():
        def tr(jj, kn2):
            start = pl.multiple_of(jj * kb, kb)
            vt_ref[jj] = v_ref[pl.ds(start, kb), :].astype(F32).T.astype(BF16)
            kf = k_ref[pl.ds(start, kb), :].astype(F32)
            sq = kf * kf
            for mp in range(2):
                rows = jnp.sum(sq[:, mp * d:(mp + 1) * d], axis=1, keepdims=True)
                kn2 = jnp.maximum(kn2, jnp.max(rows, axis=0, keepdims=True))
            return kn2
        kn2 = lax.fori_loop(0, nblk, tr, jnp.zeros((1, 1), F32))
        kn_ref[...] = jnp.broadcast_to(kn2, kn_ref.shape)

    def split3(x):
        hi = x.astype(BF16).astype(F32)
        mid = (x - hi).astype(BF16).astype(F32)
        return hi, mid, x - hi - mid

    q = q_ref[...].astype(F32) * (LOG2E / math.sqrt(d))
    qt, qrow2 = [], []
    for mp in range(2):
        qtf = q[:, mp * d:(mp + 1) * d].T
        qrow2.append(jnp.sum(qtf * qtf, axis=0, keepdims=True))
        qt.append(qtf.astype(BF16))
    qn2 = jnp.max(jnp.maximum(qrow2[0], qrow2[1]), axis=1, keepdims=True)
    kmax = jnp.sqrt(kn_ref[0:1, 0:1]) * NORM_BOUND_SLACK

    bound = jnp.sqrt(qn2) * kmax
    reach = (2.0 * bound + UNDERFLOW_LOG2) / c
    q0 = (i * qb).astype(F32)
    seq_last = float(k_ref.shape[0] - 1)
    first = jnp.floor(jnp.maximum(q0 - reach, 0.0) * (1.0 / kb)).astype(jnp.int32)[0, 0]
    last = jnp.floor(jnp.minimum(q0 + (qb - 1) + reach, seq_last) * (1.0 / kb)).astype(jnp.int32)[0, 0]
    j_lo = jnp.minimum(first, i * ratio)
    j_hi = jnp.maximum(last + 1, (i + 1) * ratio)
    fixed_ok = (2.0 * bound <= FIXED_MAX_RANGE).astype(jnp.int32)[0, 0]

    acc_ref[...] = jnp.zeros_like(acc_ref)
    qw = st_ref.shape[1]
    nq = qb // qw
    chains = [(mp, hf) for mp in range(2) for hf in range(nq)]
    row = lax.broadcasted_iota(jnp.int32, (d, qb), 0)
    lane = lax.broadcasted_iota(jnp.int32, (kb, d), 1)
    cr = c * lax.broadcasted_iota(jnp.int32, (d, qb), 1).astype(F32)
    upos = lax.broadcasted_iota(jnp.int32, (kb, d), 0).astype(F32)

    def query_aug(x, ones):
        x = split3(x)
        return jnp.where(row == 0, x[0], jnp.where(row == 1, x[1], jnp.where(row == 2, x[2],
                         jnp.where(row < 6, ones, 0.0)))).astype(BF16)

    def key_aug(y):
        y = split3(y)
        return jnp.where(lane < 3, -1.0, jnp.where(lane == 3, y[0], jnp.where(lane == 4, y[1],
                         jnp.where(lane == 5, y[2], 0.0)))).astype(BF16)

    def key_operand(j, mp, aug):
        start = pl.multiple_of(j * kb, kb)
        return jnp.concatenate([k_ref[pl.ds(start, kb), mp * d:(mp + 1) * d], aug], axis=1)

    def overlap_term(overlap, hf):
        ahead = (lax.broadcasted_iota(jnp.int32, (kb, qw), 0) - lax.broadcasted_iota(jnp.int32, (kb, qw), 1)
                 + (overlap - hf * qw))
        return (-2.0 * c) * jnp.maximum(ahead, 0).astype(F32)

    def run_chains(j, variant, next_variant, aug_of, finish):
        kp = [key_operand(j, mp, aug_of(j)) for mp in range(2)]
        results = {}
        pending = st_ref[...]
        for n, (mp, hf) in enumerate(chains):
            st = pending
            if n + 1 < len(chains):
                mp1, hf1 = chains[n + 1]
                pending = _dot(kp[mp1], qt_ref[2 * variant + mp1, :, hf1 * qw:(hf1 + 1) * qw])
            else:
                jn = jnp.minimum(j + 1, nblk - 1)
                st_ref[...] = _dot(key_operand(jn, 0, aug_of(jn)), qt_ref[2 * next_variant, :, 0:qw])
            results[(mp, hf)] = finish(mp, hf, st)
        return results

    def gather(results, idx):
        return [jnp.concatenate([results[(mp, hf)][idx] for hf in range(nq)], axis=1) for mp in range(2)]

    def fixed_path():
        for mp in range(2):
            m_ub = jnp.sqrt(qrow2[mp]) * kmax
            qt_ref[mp] = jnp.concatenate([qt[mp], query_aug(m_ub + cr, 1.0)], axis=0)
            qt_ref[2 + mp] = jnp.concatenate([qt[mp], query_aug(m_ub - cr, -1.0)], axis=0)

        def aug_of(j):
            return key_aug(c * ((j * kb - i * qb).astype(F32) + upos))

        def step(j, carry, variant, overlap, next_variant):
            vt = vt_ref[j]

            def finish(mp, hf, st):
                if overlap is not None:
                    st = st + overlap_term(overlap, hf)
                pt = jnp.exp2(st)
                acc_ref[mp, :, hf * qw:(hf + 1) * qw] += _dot(vt, pt.astype(BF16))
                return (jnp.sum(pt, axis=0, keepdims=True),)

            sums = gather(run_chains(j, variant, next_variant, aug_of, finish), 0)
            return (carry[0] + sums[0], carry[1] + sums[1])

        zero = jnp.zeros((1, qb), F32)
        st_ref[...] = _dot(key_operand(j_lo, 0, aug_of(j_lo)), qt_ref[0, :, 0:qw])
        carry = lax.fori_loop(j_lo, i * ratio, lambda j, cy: step(j, cy, 0, None, 0), (zero, zero))
        for jj in range(ratio):
            carry = step(i * ratio + jj, carry, 0, jj * kb, 0 if jj + 1 < ratio else 1)
        carry = lax.fori_loop((i + 1) * ratio, j_hi, lambda j, cy: step(j, cy, 1, None, 1), carry)
        l_ref[0] = carry[0]
        l_ref[1] = carry[1]

    def running_path():
        aug_k = key_aug(c * upos)
        for mp in range(2):
            qt_ref[mp] = jnp.concatenate([qt[mp], query_aug(cr, 1.0)], axis=0)
            qt_ref[2 + mp] = jnp.concatenate([qt[mp], query_aug(-cr, -1.0)], axis=0)

        def step(j, carry, variant, delta, overlap, next_variant):
            vt = vt_ref[j]

            def finish(mp, hf, st):
                if overlap is not None:
                    st = st + overlap_term(overlap, hf)
                m_old, l_old = carry[2 * mp][:, hf * qw:(hf + 1) * qw], carry[2 * mp + 1][:, hf * qw:(hf + 1) * qw]
                m_new = jnp.maximum(m_old, jnp.max(st, axis=0, keepdims=True) + delta)
                a = jnp.exp2(m_old - m_new)
                pt = jnp.exp2(st - (m_new - delta))
                l_new = a * l_old + jnp.sum(pt, axis=0, keepdims=True)
                acc_ref[mp, :, hf * qw:(hf + 1) * qw] = (a * acc_ref[mp, :, hf * qw:(hf + 1) * qw]
                                                         + _dot(vt, pt.astype(BF16)))
                return m_new, l_new

            res = run_chains(j, variant, next_variant, lambda jx: aug_k, finish)
            m_new, l_new = gather(res, 0), gather(res, 1)
            return (m_new[0], l_new[0], m_new[1], l_new[1])

        def below(j, carry):
            return step(j, carry, 0, -c * (i * qb - j * kb).astype(F32), None, 0)

        def above(j, carry):
            return step(j, carry, 1, -c * (j * kb - i * qb).astype(F32), None, 1)

        neg = jnp.full((1, qb), -1e30, F32)
        zero = jnp.zeros((1, qb), F32)
        st_ref[...] = _dot(key_operand(j_lo, 0, aug_k), qt_ref[0, :, 0:qw])
        carry = lax.fori_loop(j_lo, i * ratio, below, (neg, zero, neg, zero))
        for jj in range(ratio):
            carry = step(i * ratio + jj, carry, 0, c * float(jj * kb), jj * kb, 0 if jj + 1 < ratio else 1)
        carry = lax.fori_loop((i + 1) * ratio, j_hi, above, carry)
        l_ref[0] = carry[1]
        l_ref[1] = carry[3]

    pl.when(fixed_ok == 1)(fixed_path)
    pl.when(fixed_ok == 0)(running_path)
    l0, l1 = l_ref[0], l_ref[1]

    dl = dl_ref[...]
    lam = (jnp.exp(jnp.sum(dl[0:1] * dl[1:2], axis=-1, keepdims=True))
           - jnp.exp(jnp.sum(dl[2:3] * dl[3:4], axis=-1, keepdims=True)) + lam_init)
    out = (acc_ref[0] / l0 - lam * (acc_ref[1] / l1)).T
    ms = jnp.mean(out * out, axis=-1, keepdims=True)
    out = out * lax.rsqrt(ms + EPS) * subg_ref[...] * (1.0 - lam_init)
    o_ref[...] = (out * _silu(g_ref[...].astype(F32))).astype(BF16)


def _attention(proj3, diff_lambda, subln_g, *, lam_init, qb, kb):
    b, t, _ = proj3.shape
    hh = ATT_HEADS
    tq = qb
    kern = functools.partial(_attn_kernel, qb=qb, kb=kb, lam_init=lam_init)
    return pl.pallas_call(
        kern,
        out_shape=jax.ShapeDtypeStruct((b, t, hh * ATT_V_DIM), BF16),
        grid=(b, hh, t // tq),
        in_specs=[
            pl.BlockSpec((4, ATT_HEAD_DIM), lambda bi, h, i: (0, 0)),
            pl.BlockSpec((1, ATT_V_DIM), lambda bi, h, i: (0, 0)),
            pl.BlockSpec((None, tq, ATT_V_DIM), lambda bi, h, i: (bi, i, h)),
            pl.BlockSpec((None, t, ATT_V_DIM), lambda bi, h, i: (bi, 0, hh + h)),
            pl.BlockSpec((None, t, ATT_V_DIM), lambda bi, h, i: (bi, 0, 2 * hh + h)),
            pl.BlockSpec((None, tq, ATT_V_DIM), lambda bi, h, i: (bi, i, 3 * hh + h)),
        ],
        out_specs=pl.BlockSpec((None, tq, ATT_V_DIM), lambda bi, h, i: (bi, i, h)),
        scratch_shapes=[
            pltpu.VMEM((2, ATT_V_DIM, qb), F32),
            pltpu.VMEM((4, 2 * ATT_HEAD_DIM, qb), BF16),
            pltpu.VMEM((t // kb, ATT_V_DIM, kb), BF16),
            pltpu.VMEM((kb, min(qb, 2 * LANES)), F32),
            pltpu.VMEM((8, LANES), F32),
            pltpu.VMEM((2, 1, qb), F32),
        ],
        compiler_params=_cparams(("parallel", "parallel", "arbitrary")),
        name="diff_attn",
    )(diff_lambda, subln_g, proj3, proj3, proj3, proj3)


def _conv_kernel(cur_ref, prev_ref, next_ref, w_ref, b_ref, o_ref, u_ref, *, tt):
    i = pl.program_id(1)
    halo = CONV_HALO_ROWS
    pad = (D_CONV - 1) // 2
    prev = prev_ref[...].astype(F32)
    nxt = next_ref[...].astype(F32)
    u_ref[0:halo, :] = jnp.where(i > 0, prev, 0.0)
    u_ref[halo:halo + tt, :] = cur_ref[...].astype(F32)
    u_ref[halo + tt:2 * halo + tt, :] = jnp.where(i < pl.num_programs(1) - 1, nxt, 0.0)
    u = u_ref[...]
    n = tt + 2 * halo
    down = lambda x: pltpu.roll(x, 1, 0)
    up = lambda x: pltpu.roll(x, n - 1, 0)
    before = u * w_ref[0:1, :]
    for k in range(1, pad):
        before = down(before) + u * w_ref[k:k + 1, :]
    after = u * w_ref[D_CONV - 1:D_CONV, :]
    for k in range(D_CONV - 2, pad, -1):
        after = up(after) + u * w_ref[k:k + 1, :]
    acc = u * w_ref[pad:pad + 1, :] + down(before) + up(after)
    o_ref[...] = _silu(acc[halo:halo + tt] + b_ref[...]).astype(BF16)


def _conv(proj3, conv_w, conv_b, *, col0, tt, tc):
    b, t, _ = proj3.shape
    width = conv_w.shape[1]
    halo = CONV_HALO_ROWS
    cb0 = col0 // tc
    rpb = tt // halo
    last = t // halo - 1
    kern = functools.partial(_conv_kernel, tt=tt)
    return pl.pallas_call(
        kern,
        out_shape=jax.ShapeDtypeStruct((b, t, width), BF16),
        grid=(b, t // tt, width // tc),
        in_specs=[
            pl.BlockSpec((None, tt, tc), lambda bi, i, c: (bi, i, cb0 + c)),
            pl.BlockSpec((None, halo, tc), lambda bi, i, c: (bi, jnp.maximum(i * rpb - 1, 0), cb0 + c)),
            pl.BlockSpec((None, halo, tc), lambda bi, i, c: (bi, jnp.minimum((i + 1) * rpb, last), cb0 + c)),
            pl.BlockSpec((D_CONV, tc), lambda bi, i, c: (0, c)),
            pl.BlockSpec((1, tc), lambda bi, i, c: (0, c)),
        ],
        out_specs=pl.BlockSpec((None, tt, tc), lambda bi, i, c: (bi, i, c)),
        scratch_shapes=[pltpu.VMEM((tt + 2 * halo, tc), F32)],
        compiler_params=_cparams(("parallel", "parallel", "parallel")),
        name="conv_silu",
    )(proj3, proj3, proj3, conv_w, conv_b)


def _cumsum_rows(x):
    row = lax.broadcasted_iota(jnp.int32, x.shape, 0)
    sh = 1
    while sh < x.shape[0]:
        x = x + jnp.where(row >= sh, pltpu.roll(x, sh, 0), 0.0)
        sh *= 2
    return x


def _ssd_direction(xa_ref, dtr_ref, dtb_ref, alog_ref, r_ref, s_ref, y_ref, dskip_ref, *, head0, backward):
    L = CHUNK
    d_ssm = SSM_HEADS * SSM_HEAD_DIM
    n_bc = SSM_GROUPS * SSM_STATE
    dt = jax.nn.softplus(dtr_ref[...] + dtb_ref[...])
    a = dt * (-jnp.exp(alog_ref[...]))
    p = _cumsum_rows(a)
    tot = p[L - 1:L, :]
    ac = (tot - p + a) if backward else p
    eo = jnp.exp(ac)
    ds = jnp.exp(tot - ac)
    cd = jnp.broadcast_to(jnp.exp(tot), (8, LANES))
    stack = jnp.concatenate([dt, eo, ds, cd], axis=0)
    ex = _dot(stack.astype(BF16), r_ref[...])
    dtx, eox, dsx, cdx = ex[0:L], ex[L:2 * L], ex[2 * L:3 * L], ex[3 * L:3 * L + 1]

    xs = xa_ref[:, 0:d_ssm].astype(F32)
    xd = xs * dtx
    xdb = xd.astype(BF16)
    wst = (xd * dsx).astype(BF16)
    ac_t = ac.T
    row = lax.broadcasted_iota(jnp.int32, (L, L), 0)
    col = lax.broadcasted_iota(jnp.int32, (L, L), 1)
    mask = (row <= col) if backward else (row >= col)
    lane_lo = col < SSM_HEAD_DIM
    zero_b = jnp.zeros((L, L), BF16)

    for g in range(SSM_GROUPS):
        c0 = g * GROUP_WIDTH
        bg = xa_ref[:, d_ssm + g * SSM_STATE:d_ssm + (g + 1) * SSM_STATE]
        cg = xa_ref[:, d_ssm + n_bc + g * SSM_STATE:d_ssm + n_bc + (g + 1) * SSM_STATE]
        cb = _dot_nt(cg, bg)
        st = s_ref[g]
        y = _dot(cg, st.astype(BF16)) * eox[:, c0:c0 + GROUP_WIDTH]
        if dskip_ref is not None:
            y = y + xs[:, c0:c0 + GROUP_WIDTH] * dskip_ref[:, c0:c0 + GROUP_WIDTH]
        pieces = []
        for pp in range(HEADS_PER_GROUP // 2):
            ms = []
            for e in (2 * pp, 2 * pp + 1):
                hc = head0 + g * HEADS_PER_GROUP + e
                seg = ac[:, hc:hc + 1] - ac_t[hc:hc + 1, :]
                lm = jnp.exp(jnp.where(mask, seg, -jnp.inf))
                ms.append((cb * lm).astype(BF16))
            xpair = xdb[:, c0 + pp * LANES:c0 + (pp + 1) * LANES]
            rhs = jnp.concatenate([jnp.where(lane_lo, xpair, zero_b), jnp.where(lane_lo, zero_b, xpair)], axis=0)
            pieces.append(_dot(jnp.concatenate(ms, axis=1), rhs))
        y = y + jnp.concatenate(pieces, axis=1)
        y_ref[:, c0:c0 + GROUP_WIDTH] = y.astype(BF16)
        bt = bg.astype(F32).T.astype(BF16)
        s_ref[g] = st * cdx[:, c0:c0 + GROUP_WIDTH] + _dot(bt, wst[:, c0:c0 + GROUP_WIDTH])


def _ssd_kernel(dtb_ref, alog_ref, dskip_ref, rf_ref, rb_ref, xaf_ref, dtf_ref, xab_ref, dtbk_ref,
                yf_ref, yb_ref, sf_ref, sb_ref):
    @pl.when(pl.program_id(1) == 0)
    def _():
        sf_ref[...] = jnp.zeros_like(sf_ref)
        sb_ref[...] = jnp.zeros_like(sb_ref)

    _ssd_direction(xaf_ref, dtf_ref, dtb_ref, alog_ref, rf_ref, sf_ref, yf_ref, dskip_ref,
                   head0=0, backward=False)
    _ssd_direction(xab_ref, dtbk_ref, dtb_ref, alog_ref, rb_ref, sb_ref, yb_ref, None,
                   head0=SSM_HEADS, backward=True)


def _ssd(xbc_act, dt_raw3, dtb_row, alog_row, dskip_row, r_f, r_b):
    b, t, wa = xbc_act.shape
    nc = t // CHUNK
    d_ssm = SSM_HEADS * SSM_HEAD_DIM
    const = lambda bi, c: (0, 0)
    return pl.pallas_call(
        _ssd_kernel,
        out_shape=(jax.ShapeDtypeStruct((b, t, d_ssm), BF16), jax.ShapeDtypeStruct((b, t, d_ssm), BF16)),
        grid=(b, nc),
        in_specs=[
            pl.BlockSpec((1, LANES), const),
            pl.BlockSpec((1, LANES), const),
            pl.BlockSpec((1, d_ssm), const),
            pl.BlockSpec((LANES, d_ssm), const),
            pl.BlockSpec((LANES, d_ssm), const),
            pl.BlockSpec((None, CHUNK, wa), lambda bi, c: (bi, c, 0)),
            pl.BlockSpec((None, CHUNK, LANES), lambda bi, c: (bi, c, 0)),
            pl.BlockSpec((None, CHUNK, wa), lambda bi, c: (bi, nc - 1 - c, 0)),
            pl.BlockSpec((None, CHUNK, LANES), lambda bi, c: (bi, nc - 1 - c, 0)),
        ],
        out_specs=(
            pl.BlockSpec((None, CHUNK, d_ssm), lambda bi, c: (bi, c, 0)),
            pl.BlockSpec((None, CHUNK, d_ssm), lambda bi, c: (bi, nc - 1 - c, 0)),
        ),
        scratch_shapes=[
            pltpu.VMEM((SSM_GROUPS, SSM_STATE, GROUP_WIDTH), F32),
            pltpu.VMEM((SSM_GROUPS, SSM_STATE, GROUP_WIDTH), F32),
        ],
        compiler_params=_cparams(("parallel", "arbitrary")),
        name="ssd_scan",
    )(dtb_row, alog_row, dskip_row, r_f, r_b, xbc_act, dt_raw3, xbc_act, dt_raw3)


def _mem_kv_kernel(mem_ref, w_ref, o_ref):
    o_ref[...] = _dot(mem_ref[...].astype(BF16), w_ref[...]).astype(BF16)


def _mem_kv(mem, w_kv):
    b, mt, d = mem.shape
    n = w_kv.shape[1]
    return pl.pallas_call(
        _mem_kv_kernel,
        out_shape=jax.ShapeDtypeStruct((b, mt, n), BF16),
        grid=(b,),
        in_specs=[
            pl.BlockSpec((None, mt, d), lambda bi: (bi, 0, 0)),
            pl.BlockSpec((d, n), lambda bi: (0, 0)),
        ],
        out_specs=pl.BlockSpec((None, mt, n), lambda bi: (bi, 0, 0)),
        compiler_params=_cparams(("parallel",)),
        name="mem_kv_proj",
    )(mem, w_kv)


def _mem_attn_kernel(q_ref, g_ref, kv_ref, o_ref):
    d = MEM_HEAD_DIM
    d_mem = MEM_HEADS * d
    scale = 1.0 / math.sqrt(d)
    for hh in range(MEM_HEADS):
        q = q_ref[:, hh * d:(hh + 1) * d]
        k = kv_ref[:, hh * d:(hh + 1) * d]
        v = kv_ref[:, d_mem + hh * d:d_mem + (hh + 1) * d]
        s = _dot_nt(q, k) * scale
        p = jnp.exp(s - jnp.max(s, axis=-1, keepdims=True))
        o = _dot(p.astype(BF16), v) / jnp.sum(p, axis=-1, keepdims=True)
        o_ref[:, hh * d:(hh + 1) * d] = (o * _silu(g_ref[:, hh * d:(hh + 1) * d].astype(F32))).astype(BF16)


def _mem_attn(proj3, kv, *, q_col0, tq):
    b, t, _ = proj3.shape
    mt, n = kv.shape[1:]
    d_mem = MEM_HEADS * MEM_HEAD_DIM
    qb = q_col0 // d_mem
    return pl.pallas_call(
        _mem_attn_kernel,
        out_shape=jax.ShapeDtypeStruct((b, t, d_mem), BF16),
        grid=(b, t // tq),
        in_specs=[
            pl.BlockSpec((None, tq, d_mem), lambda bi, i: (bi, i, qb)),
            pl.BlockSpec((None, tq, d_mem), lambda bi, i: (bi, i, qb + 1)),
            pl.BlockSpec((None, mt, n), lambda bi, i: (bi, 0, 0)),
        ],
        out_specs=pl.BlockSpec((None, tq, d_mem), lambda bi, i: (bi, i, 0)),
        compiler_params=_cparams(("parallel", "parallel")),
        name="mem_attn",
    )(proj3, proj3, kv)


def _out_proj_kernel(x_ref, gi_ref, bi_ref, ha_ref, yf_ref, yb_ref, z_ref, ng_ref, hm_ref, w_ref, g_ref, b_ref,
                     o_ref):
    da = ha_ref.shape[1]
    dsm = yf_ref.shape[1]
    out = _dot(ha_ref[...], w_ref[0:da, :])
    out = out + _dot(hm_ref[...], w_ref[da + dsm:, :])
    for g in range(SSM_GROUPS):
        c0 = g * GROUP_WIDTH
        yg = ((yf_ref[:, c0:c0 + GROUP_WIDTH].astype(F32) + yb_ref[:, c0:c0 + GROUP_WIDTH].astype(F32))
              * _silu(z_ref[:, c0:c0 + GROUP_WIDTH].astype(F32)))
        ms = jnp.mean(yg * yg, axis=-1, keepdims=True)
        hg = (yg * lax.rsqrt(ms + EPS) * ng_ref[:, c0:c0 + GROUP_WIDTH]).astype(BF16)
        out = out + _dot(hg, w_ref[da + c0:da + c0 + GROUP_WIDTH, :])
    xn = _layer_norm(x_ref[...], gi_ref[...], bi_ref[...])
    o_ref[...] = _layer_norm(ALPHA * xn + out, g_ref[...], b_ref[...])


def _out_proj(x2, ln_in_g, ln_in_b, h_att, y_f, y_b, proj2, norm_g, h_mem, w_out, ln_g, ln_b, *, z_col0, tm):
    m, d = x2.shape
    da, dsm, dm = h_att.shape[1], y_f.shape[1], h_mem.shape[1]
    zb = z_col0 // dsm
    row = lambda i: (i, 0)
    const = lambda i: (0, 0)
    return pl.pallas_call(
        _out_proj_kernel,
        out_shape=jax.ShapeDtypeStruct((m, d), F32),
        grid=(m // tm,),
        in_specs=[
            pl.BlockSpec((tm, d), row),
            pl.BlockSpec((1, d), const),
            pl.BlockSpec((1, d), const),
            pl.BlockSpec((tm, da), row),
            pl.BlockSpec((tm, dsm), row),
            pl.BlockSpec((tm, dsm), row),
            pl.BlockSpec((tm, dsm), lambda i: (i, zb)),
            pl.BlockSpec((1, dsm), const),
            pl.BlockSpec((tm, dm), row),
            pl.BlockSpec(w_out.shape, const, pipeline_mode=pl.Buffered(1)),
            pl.BlockSpec((1, d), const),
            pl.BlockSpec((1, d), const),
        ],
        out_specs=pl.BlockSpec((tm, d), row),
        compiler_params=_cparams(("parallel",)),
        name="out_proj_ln",
    )(x2, ln_in_g, ln_in_b, h_att, y_f, y_b, proj2, norm_g, h_mem, w_out, ln_g, ln_b)


def _tile(n, pref):
    t = min(n, pref)
    while n % t:
        t //= 2
    return t


def _trunk(x, mem, p):
    b, t, d = x.shape
    m = b * t
    x2 = x.reshape(m, d)
    proj2, dt_raw = _in_proj(x2, p["ln_in_g"], p["ln_in_b"], p["w_main"], p["w_dt"],
                             tm=_tile(m, 1024), tn=512)
    n = proj2.shape[1]
    proj3 = proj2.reshape(b, t, n)
    d_att = ATT_HEADS * ATT_V_DIM
    d_ssm = SSM_HEADS * SSM_HEAD_DIM

    h_att = _attention(proj3, p["diff_lambda"], p["subln_g"], lam_init=p["lam_init"],
                       qb=_tile(t, 2048), kb=_tile(t, 512))

    xbc_act = _conv(proj3, p["conv_w"], p["conv_b"], col0=4 * d_att + d_ssm, tt=_tile(t, 512), tc=1024)
    y_f, y_b = _ssd(xbc_act, dt_raw.reshape(b, t, LANES), p["dtb_row"], p["alog_row"], p["dskip_row"],
                    p["r_f"], p["r_b"])

    kv = _mem_kv(mem, p["w_mem_kv"])
    h_mem = _mem_attn(proj3, kv, q_col0=4 * d_att + d_ssm + p["conv_w"].shape[1], tq=_tile(t, 512))

    y = _out_proj(x2, p["ln_in_g"], p["ln_in_b"], h_att.reshape(m, d_att), y_f.reshape(m, d_ssm),
                  y_b.reshape(m, d_ssm), proj2, p["ssm_norm_g"], h_mem.reshape(m, MEM_HEADS * MEM_HEAD_DIM),
                  p["w_out"], p["ln_g"], p["ln_b"], z_col0=4 * d_att, tm=_tile(m, 256))
    return y.reshape(b, t, d)


def _expansion_matrix(head0):
    rows = lax.broadcasted_iota(jnp.int32, (LANES, SSM_HEADS * SSM_HEAD_DIM), 0)
    cols = lax.broadcasted_iota(jnp.int32, (LANES, SSM_HEADS * SSM_HEAD_DIM), 1)
    return (rows == head0 + cols // SSM_HEAD_DIM).astype(BF16)


def kernel(x_prompt, x_sample, mem_prompt, mem_sample, ln_in_g, ln_in_b, w_in, conv_w, conv_b, dt_bias, a_log, d_skip, ssm_norm_g, diff_lambda, subln_g, w_mem_kv, w_out, ln_g, ln_b):
    assert w_in.shape[0] == DEPTH
    d_att = ATT_HEADS * ATT_V_DIM
    d_ssm = SSM_HEADS * SSM_HEAD_DIM
    d_xbc = conv_w.shape[-1]
    dt0 = 4 * d_att + d_ssm + d_xbc
    dt1 = dt0 + 2 * SSM_HEADS
    row = lambda v: v.reshape(1, -1).astype(F32)
    pad_heads = lambda v: jnp.pad(v.reshape(1, -1).astype(F32), ((0, 0), (0, LANES - 2 * SSM_HEADS)))

    l = 0
    w = w_in[l]
    p = {
        "ln_in_g": row(ln_in_g), "ln_in_b": row(ln_in_b),
        "w_main": jnp.concatenate([w[:, :dt0], w[:, dt1:]], axis=1).astype(BF16),
        "w_dt": jnp.pad(w[:, dt0:dt1], ((0, 0), (0, LANES - 2 * SSM_HEADS))).astype(BF16),
        "conv_w": conv_w[l].astype(F32), "conv_b": row(conv_b[l]),
        "dtb_row": pad_heads(dt_bias[l]), "alog_row": pad_heads(a_log[l]),
        "dskip_row": row(jnp.repeat(d_skip[l], SSM_HEAD_DIM)),
        "r_f": _expansion_matrix(0), "r_b": _expansion_matrix(SSM_HEADS),
        "ssm_norm_g": row(ssm_norm_g[l]),
        "diff_lambda": diff_lambda[l].astype(F32), "subln_g": row(subln_g[l]),
        "lam_init": 0.8 - 0.6 * math.exp(-0.3 * l),
        "w_mem_kv": w_mem_kv[l].astype(BF16), "w_out": w_out[l].astype(BF16),
        "ln_g": row(ln_g[l]), "ln_b": row(ln_b[l]),
    }
    return (_trunk(x_prompt, mem_prompt, p), _trunk(x_sample, mem_sample, p))
```

```python
import functools
import math

import jax
import jax.numpy as jnp
from jax import lax
from jax.experimental import pallas as pl
from jax.experimental.pallas import tpu as pltpu

F32 = jnp.float32
BF16 = jnp.bfloat16

DEPTH = 1
EPS = 1e-5
ATT_HEADS = 8
ATT_HEAD_DIM = 128
ATT_V_DIM = 2 * ATT_HEAD_DIM
ATT_STEEP_HEADS = 3
SSM_HEADS = 32
SSM_HEAD_DIM = 64
SSM_GROUPS = 4
SSM_STATE = 128
HEADS_PER_GROUP = SSM_HEADS // SSM_GROUPS
GROUP_WIDTH = HEADS_PER_GROUP * SSM_HEAD_DIM
D_CONV = 5
CHUNK = 128
MEM_HEADS = 4
MEM_HEAD_DIM = 128
LANES = 128
LOG2E = 1.4426950408889634
UNDERFLOW_LOG2 = 130.0
NORM_BOUND_SLACK = 1.01
FIXED_MAX_RANGE = 64.0
CONV_HALO_ROWS = 16
ALPHA = (2.0 * DEPTH) ** 0.25
VMEM_LIMIT_BYTES = 56 * 1024 * 1024


def _cparams(sem):
    return pltpu.CompilerParams(dimension_semantics=sem, vmem_limit_bytes=VMEM_LIMIT_BYTES)


def _layer_norm(x, g, b):
    mu = jnp.mean(x, axis=-1, keepdims=True)
    xc = x - mu
    var = jnp.mean(xc * xc, axis=-1, keepdims=True)
    return xc * lax.rsqrt(var + EPS) * g + b


def _silu(x):
    return x * jax.nn.sigmoid(x)


def _dot(a, b):
    return jnp.dot(a, b, preferred_element_type=F32)


def _dot_nt(a, b):
    return lax.dot_general(a, b, (((1,), (1,)), ((), ())), preferred_element_type=F32)


def _in_proj_kernel(x_ref, g_ref, b_ref, w_ref, wdt_ref, proj_ref, dt_ref, xn_ref):
    @pl.when(pl.program_id(1) == 0)
    def _():
        xn = _layer_norm(x_ref[...], g_ref[...], b_ref[...]).astype(BF16)
        xn_ref[...] = xn
        dt_ref[...] = _dot(xn, wdt_ref[...])

    proj_ref[...] = _dot(xn_ref[...], w_ref[...]).astype(BF16)


def _in_proj(x2, ln_g, ln_b, w_main, w_dt, *, tm, tn):
    m, d = x2.shape
    n = w_main.shape[1]
    return pl.pallas_call(
        _in_proj_kernel,
        out_shape=(jax.ShapeDtypeStruct((m, n), BF16), jax.ShapeDtypeStruct((m, LANES), F32)),
        grid=(m // tm, n // tn),
        in_specs=[
            pl.BlockSpec((tm, d), lambda i, j: (i, 0)),
            pl.BlockSpec((1, d), lambda i, j: (0, 0)),
            pl.BlockSpec((1, d), lambda i, j: (0, 0)),
            pl.BlockSpec((d, tn), lambda i, j: (0, j)),
            pl.BlockSpec((d, LANES), lambda i, j: (0, 0)),
        ],
        out_specs=(
            pl.BlockSpec((tm, tn), lambda i, j: (i, j)),
            pl.BlockSpec((tm, LANES), lambda i, j: (i, 0)),
        ),
        scratch_shapes=[pltpu.VMEM((tm, d), BF16)],
        compiler_params=_cparams(("parallel", "arbitrary")),
        name="in_proj",
    )(x2, ln_g, ln_b, w_main, w_dt)


def _attn_kernel(dl_ref, subg_ref, q_ref, k_ref, v_ref, g_ref, o_ref, acc_ref, qt_ref, vt_ref, st_ref, kn_ref, l_ref,
                 *, qb, kb, head0, lam_init):
    h = head0 + pl.program_id(1)
    i = pl.program_id(2)
    nblk = k_ref.shape[0] // kb
    ratio = qb // kb
    d = ATT_HEAD_DIM
    c = jnp.exp2(-(h + 1).astype(F32)) * LOG2E

    @pl.when(i == 0)
    def _():
        def tr(jj, kn2):
            start = pl.multiple_of(jj * kb, kb)
            vt_ref[jj] = v_ref[pl.ds(start, kb), :].astype(F32).T.astype(BF16)
            kf = k_ref[pl.ds(start, kb), :].astype(F32)
            sq = kf * kf
            for mp in range(2):
                rows = jnp.sum(sq[:, mp * d:(mp + 1) * d], axis=1, keepdims=True)
                kn2 = jnp.maximum(kn2, jnp.max(rows, axis=0, keepdims=True))
            return kn2
        kn2 = lax.fori_loop(0, nblk, tr, jnp.zeros((1, 1), F32))
        kn_ref[...] = jnp.broadcast_to(kn2, kn_ref.shape)

    def split3(x):
        hi = x.astype(BF16).astype(F32)
        mid = (x - hi).astype(BF16).astype(F32)
        return hi, mid, x - hi - mid

    q = q_ref[...].astype(F32) * (LOG2E / math.sqrt(d))
    qt, qrow2 = [], []
    for mp in range(2):
        qtf = q[:, mp * d:(mp + 1) * d].T
        qrow2.append(jnp.sum(qtf * qtf, axis=0, keepdims=True))
        qt.append(qtf.astype(BF16))
    qn2 = jnp.max(jnp.maximum(qrow2[0], qrow2[1]), axis=1, keepdims=True)
    kmax = jnp.sqrt(kn_ref[0:1, 0:1]) * NORM_BOUND_SLACK

    bound = jnp.sqrt(qn2) * kmax
    reach = (2.0 * bound + UNDERFLOW_LOG2) / c
    q0 = (i * qb).astype(F32)
    seq_last = float(k_ref.shape[0] - 1)
    first = jnp.floor(jnp.maximum(q0 - reach, 0.0) * (1.0 / kb)).astype(jnp.int32)[0, 0]
    last = jnp.floor(jnp.minimum(q0 + (qb - 1) + reach, seq_last) * (1.0 / kb)).astype(jnp.int32)[0, 0]
    j_lo = jnp.minimum(first, i * ratio)
    j_hi = jnp.maximum(last + 1, (i + 1) * ratio)
    fixed_ok = (2.0 * bound <= FIXED_MAX_RANGE).astype(jnp.int32)[0, 0]

    acc_ref[...] = jnp.zeros_like(acc_ref)
    qw = st_ref.shape[1]
    nq = qb // qw
    chains = [(mp, hf) for mp in range(2) for hf in range(nq)]
    row = lax.broadcasted_iota(jnp.int32, (d, qb), 0)
    lane = lax.broadcasted_iota(jnp.int32, (kb, d), 1)
    cr = c * lax.broadcasted_iota(jnp.int32, (d, qb), 1).astype(F32)
    upos = lax.broadcasted_iota(jnp.int32, (kb, d), 0).astype(F32)

    def query_aug(x, ones):
        x = split3(x)
        return jnp.where(row == 0, x[0], jnp.where(row == 1, x[1], jnp.where(row == 2, x[2],
                         jnp.where(row < 6, ones, 0.0)))).astype(BF16)

    def key_aug(y):
        y = split3(y)
        return jnp.where(lane < 3, -1.0, jnp.where(lane == 3, y[0], jnp.where(lane == 4, y[1],
                         jnp.where(lane == 5, y[2], 0.0)))).astype(BF16)

    def key_operand(j, mp, aug):
        start = pl.multiple_of(j * kb, kb)
        return jnp.concatenate([k_ref[pl.ds(start, kb), mp * d:(mp + 1) * d], aug], axis=1)

    def overlap_term(overlap, hf):
        ahead = (lax.broadcasted_iota(jnp.int32, (kb, qw), 0) - lax.broadcasted_iota(jnp.int32, (kb, qw), 1)
                 + (overlap - hf * qw))
        return (-2.0 * c) * jnp.maximum(ahead, 0).astype(F32)

    def run_chains(j, variant, next_variant, aug_of, finish):
        kp = [key_operand(j, mp, aug_of(j)) for mp in range(2)]
        results = {}
        pending = st_ref[...]
        for n, (mp, hf) in enumerate(chains):
            st = pending
            if n + 1 < len(chains):
                mp1, hf1 = chains[n + 1]
                pending = _dot(kp[mp1], qt_ref[2 * variant + mp1, :, hf1 * qw:(hf1 + 1) * qw])
            else:
                jn = jnp.minimum(j + 1, nblk - 1)
                st_ref[...] = _dot(key_operand(jn, 0, aug_of(jn)), qt_ref[2 * next_variant, :, 0:qw])
            results[(mp, hf)] = finish(mp, hf, st)
        return results

    def gather(results, idx):
        return [jnp.concatenate([results[(mp, hf)][idx] for hf in range(nq)], axis=1) for mp in range(2)]

    def fixed_path():
        for mp in range(2):
            m_ub = jnp.sqrt(qrow2[mp]) * kmax
            qt_ref[mp] = jnp.concatenate([qt[mp], query_aug(m_ub + cr, 1.0)], axis=0)
            qt_ref[2 + mp] = jnp.concatenate([qt[mp], query_aug(m_ub - cr, -1.0)], axis=0)

        def aug_of(j):
            return key_aug(c * ((j * kb - i * qb).astype(F32) + upos))

        def step(j, carry, variant, overlap, next_variant):
            vt = vt_ref[j]

            def finish(mp, hf, st):
                if overlap is not None:
                    st = st + overlap_term(overlap, hf)
                pt = jnp.exp2(st)
                acc_ref[mp, :, hf * qw:(hf + 1) * qw] += _dot(vt, pt.astype(BF16))
                return (jnp.sum(pt, axis=0, keepdims=True),)

            sums = gather(run_chains(j, variant, next_variant, aug_of, finish), 0)
            return (carry[0] + sums[0], carry[1] + sums[1])

        zero = jnp.zeros((1, qb), F32)
        st_ref[...] = _dot(key_operand(j_lo, 0, aug_of(j_lo)), qt_ref[0, :, 0:qw])
        carry = lax.fori_loop(j_lo, i * ratio, lambda j, cy: step(j, cy, 0, None, 0), (zero, zero))
        for jj in range(ratio):
            carry = step(i * ratio + jj, carry, 0, jj * kb, 0 if jj + 1 < ratio else 1)
        carry = lax.fori_loop((i + 1) * ratio, j_hi, lambda j, cy: step(j, cy, 1, None, 1), carry)
        l_ref[0] = carry[0]
        l_ref[1] = carry[1]

    def running_path():
        aug_k = key_aug(c * upos)
        for mp in range(2):
            qt_ref[mp] = jnp.concatenate([qt[mp], query_aug(cr, 1.0)], axis=0)
            qt_ref[2 + mp] = jnp.concatenate([qt[mp], query_aug(-cr, -1.0)], axis=0)

        def step(j, carry, variant, delta, overlap, next_variant):
            vt = vt_ref[j]

            def finish(mp, hf, st):
                if overlap is not None:
                    st = st + overlap_term(overlap, hf)
                m_old, l_old = carry[2 * mp][:, hf * qw:(hf + 1) * qw], carry[2 * mp + 1][:, hf * qw:(hf + 1) * qw]
                m_new = jnp.maximum(m_old, jnp.max(st, axis=0, keepdims=True) + delta)
                a = jnp.exp2(m_old - m_new)
                pt = jnp.exp2(st - (m_new - delta))
                l_new = a * l_old + jnp.sum(pt, axis=0, keepdims=True)
                acc_ref[mp, :, hf * qw:(hf + 1) * qw] = (a * acc_ref[mp, :, hf * qw:(hf + 1) * qw]
                                                         + _dot(vt, pt.astype(BF16)))
                return m_new, l_new

            res = run_chains(j, variant, next_variant, lambda jx: aug_k, finish)
            m_new, l_new = gather(res, 0), gather(res, 1)
            return (m_new[0], l_new[0], m_new[1], l_new[1])

        def below(j, carry):
            return step(j, carry, 0, -c * (i * qb - j * kb).astype(F32), None, 0)

        def above(j, carry):
            return step(j, carry, 1, -c * (j * kb - i * qb).astype(F32), None, 1)

        neg = jnp.full((1, qb), -1e30, F32)
        zero = jnp.zeros((1, qb), F32)
        st_ref[...] = _dot(key_operand(j_lo, 0, aug_k), qt_ref[0, :, 0:qw])
        carry = lax.fori_loop(j_lo, i * ratio, below, (neg, zero, neg, zero))
        for jj in range(ratio):
            carry = step(i * ratio + jj, carry, 0, c * float(jj * kb), jj * kb, 0 if jj + 1 < ratio else 1)
        carry = lax.fori_loop((i + 1) * ratio, j_hi, above, carry)
        l_ref[0] = carry[1]
        l_ref[1] = carry[3]

    pl.when(fixed_ok == 1)(fixed_path)
    pl.when(fixed_ok == 0)(running_path)
    l0, l1 = l_ref[0], l_ref[1]

    dl = dl_ref[...]
    lam = (jnp.exp(jnp.sum(dl[0:1] * dl[1:2], axis=-1, keepdims=True))
           - jnp.exp(jnp.sum(dl[2:3] * dl[3:4], axis=-1, keepdims=True)) + lam_init)
    out = (acc_ref[0] / l0 - lam * (acc_ref[1] / l1)).T
    ms = jnp.mean(out * out, axis=-1, keepdims=True)
    out = out * lax.rsqrt(ms + EPS) * subg_ref[...] * (1.0 - lam_init)
    o_ref[...] = (out * _silu(g_ref[...].astype(F32))).astype(BF16)


def _attention(proj3, diff_lambda, subln_g, *, lam_init, head0, nheads, qb, kb):
    b, t, _ = proj3.shape
    hh = ATT_HEADS
    tq = qb
    kern = functools.partial(_attn_kernel, qb=qb, kb=kb, head0=head0, lam_init=lam_init)
    return pl.pallas_call(
        kern,
        out_shape=jax.ShapeDtypeStruct((b, t, nheads * ATT_V_DIM), BF16),
        grid=(b, nheads, t // tq),
        in_specs=[
            pl.BlockSpec((4, ATT_HEAD_DIM), lambda bi, h, i: (0, 0)),
            pl.BlockSpec((1, ATT_V_DIM), lambda bi, h, i: (0, 0)),
            pl.BlockSpec((None, tq, ATT_V_DIM), lambda bi, h, i: (bi, i, head0 + h)),
            pl.BlockSpec((None, t, ATT_V_DIM), lambda bi, h, i: (bi, 0, hh + head0 + h)),
            pl.BlockSpec((None, t, ATT_V_DIM), lambda bi, h, i: (bi, 0, 2 * hh + head0 + h)),
            pl.BlockSpec((None, tq, ATT_V_DIM), lambda bi, h, i: (bi, i, 3 * hh + head0 + h)),
        ],
        out_specs=pl.BlockSpec((None, tq, ATT_V_DIM), lambda bi, h, i: (bi, i, h)),
        scratch_shapes=[
            pltpu.VMEM((2, ATT_V_DIM, qb), F32),
            pltpu.VMEM((4, 2 * ATT_HEAD_DIM, qb), BF16),
            pltpu.VMEM((t // kb, ATT_V_DIM, kb), BF16),
            pltpu.VMEM((kb, min(qb, 2 * LANES)), F32),
            pltpu.VMEM((8, LANES), F32),
            pltpu.VMEM((2, 1, qb), F32),
        ],
        compiler_params=_cparams(("parallel", "parallel", "arbitrary")),
        name="diff_attn",
    )(diff_lambda, subln_g, proj3, proj3, proj3, proj3)


def _conv_kernel(cur_ref, prev_ref, next_ref, w_ref, b_ref, o_ref, u_ref, *, tt):
    i = pl.program_id(1)
    halo = CONV_HALO_ROWS
    pad = (D_CONV - 1) // 2
    prev = prev_ref[...].astype(F32)
    nxt = next_ref[...].astype(F32)
    u_ref[0:halo, :] = jnp.where(i > 0, prev, 0.0)
    u_ref[halo:halo + tt, :] = cur_ref[...].astype(F32)
    u_ref[halo + tt:2 * halo + tt, :] = jnp.where(i < pl.num_programs(1) - 1, nxt, 0.0)
    u = u_ref[...]
    n = tt + 2 * halo
    down = lambda x: pltpu.roll(x, 1, 0)
    up = lambda x: pltpu.roll(x, n - 1, 0)
    before = u * w_ref[0:1, :]
    for k in range(1, pad):
        before = down(before) + u * w_ref[k:k + 1, :]
    after = u * w_ref[D_CONV - 1:D_CONV, :]
    for k in range(D_CONV - 2, pad, -1):
        after = up(after) + u * w_ref[k:k + 1, :]
    acc = u * w_ref[pad:pad + 1, :] + down(before) + up(after)
    o_ref[...] = _silu(acc[halo:halo + tt] + b_ref[...]).astype(BF16)


def _conv(proj3, conv_w, conv_b, *, col0, tt, tc):
    b, t, _ = proj3.shape
    width = conv_w.shape[1]
    halo = CONV_HALO_ROWS
    cb0 = col0 // tc
    rpb = tt // halo
    last = t // halo - 1
    kern = functools.partial(_conv_kernel, tt=tt)
    return pl.pallas_call(
        kern,
        out_shape=jax.ShapeDtypeStruct((b, t, width), BF16),
        grid=(b, t // tt, width // tc),
        in_specs=[
            pl.BlockSpec((None, tt, tc), lambda bi, i, c: (bi, i, cb0 + c)),
            pl.BlockSpec((None, halo, tc), lambda bi, i, c: (bi, jnp.maximum(i * rpb - 1, 0), cb0 + c)),
            pl.BlockSpec((None, halo, tc), lambda bi, i, c: (bi, jnp.minimum((i + 1) * rpb, last), cb0 + c)),
            pl.BlockSpec((D_CONV, tc), lambda bi, i, c: (0, c)),
            pl.BlockSpec((1, tc), lambda bi, i, c: (0, c)),
        ],
        out_specs=pl.BlockSpec((None, tt, tc), lambda bi, i, c: (bi, i, c)),
        scratch_shapes=[pltpu.VMEM((tt + 2 * halo, tc), F32)],
        compiler_params=_cparams(("parallel", "parallel", "parallel")),
        name="conv_silu",
    )(proj3, proj3, proj3, conv_w, conv_b)


def _cumsum_rows(x):
    row = lax.broadcasted_iota(jnp.int32, x.shape, 0)
    sh = 1
    while sh < x.shape[0]:
        x = x + jnp.where(row >= sh, pltpu.roll(x, sh, 0), 0.0)
        sh *= 2
    return x


def _ssd_direction(xa_ref, dtr_ref, dtb_ref, alog_ref, r_ref, s_ref, y_ref, dskip_ref, *, head0, backward):
    L = CHUNK
    d_ssm = SSM_HEADS * SSM_HEAD_DIM
    n_bc = SSM_GROUPS * SSM_STATE
    dt = jax.nn.softplus(dtr_ref[...] + dtb_ref[...])
    a = dt * (-jnp.exp(alog_ref[...]))
    p = _cumsum_rows(a)
    tot = p[L - 1:L, :]
    ac = (tot - p + a) if backward else p
    eo = jnp.exp(ac)
    ds = jnp.exp(tot - ac)
    cd = jnp.broadcast_to(jnp.exp(tot), (8, LANES))
    stack = jnp.concatenate([dt, eo, ds, cd], axis=0)
    ex = _dot(stack.astype(BF16), r_ref[...])
    dtx, eox, dsx, cdx = ex[0:L], ex[L:2 * L], ex[2 * L:3 * L], ex[3 * L:3 * L + 1]

    xs = xa_ref[:, 0:d_ssm].astype(F32)
    xd = xs * dtx
    xdb = xd.astype(BF16)
    wst = (xd * dsx).astype(BF16)
    ac_t = ac.T
    row = lax.broadcasted_iota(jnp.int32, (L, L), 0)
    col = lax.broadcasted_iota(jnp.int32, (L, L), 1)
    mask = (row <= col) if backward else (row >= col)
    lane_lo = col < SSM_HEAD_DIM
    zero_b = jnp.zeros((L, L), BF16)

    for g in range(SSM_GROUPS):
        c0 = g * GROUP_WIDTH
        bg = xa_ref[:, d_ssm + g * SSM_STATE:d_ssm + (g + 1) * SSM_STATE]
        cg = xa_ref[:, d_ssm + n_bc + g * SSM_STATE:d_ssm + n_bc + (g + 1) * SSM_STATE]
        cb = _dot_nt(cg, bg)
        st = s_ref[g]
        y = _dot(cg, st.astype(BF16)) * eox[:, c0:c0 + GROUP_WIDTH]
        if dskip_ref is not None:
            y = y + xs[:, c0:c0 + GROUP_WIDTH] * dskip_ref[:, c0:c0 + GROUP_WIDTH]
        pieces = []
        for pp in range(HEADS_PER_GROUP // 2):
            ms = []
            for e in (2 * pp, 2 * pp + 1):
                hc = head0 + g * HEADS_PER_GROUP + e
                seg = ac[:, hc:hc + 1] - ac_t[hc:hc + 1, :]
                lm = jnp.exp(jnp.where(mask, seg, -jnp.inf))
                ms.append((cb * lm).astype(BF16))
            xpair = xdb[:, c0 + pp * LANES:c0 + (pp + 1) * LANES]
            rhs = jnp.concatenate([jnp.where(lane_lo, xpair, zero_b), jnp.where(lane_lo, zero_b, xpair)], axis=0)
            pieces.append(_dot(jnp.concatenate(ms, axis=1), rhs))
        y = y + jnp.concatenate(pieces, axis=1)
        y_ref[:, c0:c0 + GROUP_WIDTH] = y.astype(BF16)
        bt = bg.astype(F32).T.astype(BF16)
        s_ref[g] = st * cdx[:, c0:c0 + GROUP_WIDTH] + _dot(bt, wst[:, c0:c0 + GROUP_WIDTH])


def _ssd_kernel(dtb_ref, alog_ref, dskip_ref, rf_ref, rb_ref, xaf_ref, dtf_ref, xab_ref, dtbk_ref,
                yf_ref, yb_ref, sf_ref, sb_ref):
    @pl.when(pl.program_id(1) == 0)
    def _():
        sf_ref[...] = jnp.zeros_like(sf_ref)
        sb_ref[...] = jnp.zeros_like(sb_ref)

    _ssd_direction(xaf_ref, dtf_ref, dtb_ref, alog_ref, rf_ref, sf_ref, yf_ref, dskip_ref,
                   head0=0, backward=False)
    _ssd_direction(xab_ref, dtbk_ref, dtb_ref, alog_ref, rb_ref, sb_ref, yb_ref, None,
                   head0=SSM_HEADS, backward=True)


def _ssd(xbc_act, dt_raw3, dtb_row, alog_row, dskip_row, r_f, r_b):
    b, t, wa = xbc_act.shape
    nc = t // CHUNK
    d_ssm = SSM_HEADS * SSM_HEAD_DIM
    const = lambda bi, c: (0, 0)
    return pl.pallas_call(
        _ssd_kernel,
        out_shape=(jax.ShapeDtypeStruct((b, t, d_ssm), BF16), jax.ShapeDtypeStruct((b, t, d_ssm), BF16)),
        grid=(b, nc),
        in_specs=[
            pl.BlockSpec((1, LANES), const),
            pl.BlockSpec((1, LANES), const),
            pl.BlockSpec((1, d_ssm), const),
            pl.BlockSpec((LANES, d_ssm), const),
            pl.BlockSpec((LANES, d_ssm), const),
            pl.BlockSpec((None, CHUNK, wa), lambda bi, c: (bi, c, 0)),
            pl.BlockSpec((None, CHUNK, LANES), lambda bi, c: (bi, c, 0)),
            pl.BlockSpec((None, CHUNK, wa), lambda bi, c: (bi, nc - 1 - c, 0)),
            pl.BlockSpec((None, CHUNK, LANES), lambda bi, c: (bi, nc - 1 - c, 0)),
        ],
        out_specs=(
            pl.BlockSpec((None, CHUNK, d_ssm), lambda bi, c: (bi, c, 0)),
            pl.BlockSpec((None, CHUNK, d_ssm), lambda bi, c: (bi, nc - 1 - c, 0)),
        ),
        scratch_shapes=[
            pltpu.VMEM((SSM_GROUPS, SSM_STATE, GROUP_WIDTH), F32),
            pltpu.VMEM((SSM_GROUPS, SSM_STATE, GROUP_WIDTH), F32),
        ],
        compiler_params=_cparams(("parallel", "arbitrary")),
        name="ssd_scan",
    )(dtb_row, alog_row, dskip_row, r_f, r_b, xbc_act, dt_raw3, xbc_act, dt_raw3)


def _mem_kv_kernel(mem_ref, w_ref, o_ref):
    o_ref[...] = _dot(mem_ref[...].astype(BF16), w_ref[...]).astype(BF16)


def _mem_kv(mem, w_kv):
    b, mt, d = mem.shape
    n = w_kv.shape[1]
    return pl.pallas_call(
        _mem_kv_kernel,
        out_shape=jax.ShapeDtypeStruct((b, mt, n), BF16),
        grid=(b,),
        in_specs=[
            pl.BlockSpec((None, mt, d), lambda bi: (bi, 0, 0)),
            pl.BlockSpec((d, n), lambda bi: (0, 0)),
        ],
        out_specs=pl.BlockSpec((None, mt, n), lambda bi: (bi, 0, 0)),
        compiler_params=_cparams(("parallel",)),
        name="mem_kv_proj",
    )(mem, w_kv)


def _mem_attn_kernel(q_ref, g_ref, kv_ref, o_ref):
    d = MEM_HEAD_DIM
    d_mem = MEM_HEADS * d
    scale = 1.0 / math.sqrt(d)
    for hh in range(MEM_HEADS):
        q = q_ref[:, hh * d:(hh + 1) * d]
        k = kv_ref[:, hh * d:(hh + 1) * d]
        v = kv_ref[:, d_mem + hh * d:d_mem + (hh + 1) * d]
        s = _dot_nt(q, k) * scale
        p = jnp.exp(s - jnp.max(s, axis=-1, keepdims=True))
        o = _dot(p.astype(BF16), v) / jnp.sum(p, axis=-1, keepdims=True)
        o_ref[:, hh * d:(hh + 1) * d] = (o * _silu(g_ref[:, hh * d:(hh + 1) * d].astype(F32))).astype(BF16)


def _mem_attn(proj3, kv, *, q_col0, tq):
    b, t, _ = proj3.shape
    mt, n = kv.shape[1:]
    d_mem = MEM_HEADS * MEM_HEAD_DIM
    qb = q_col0 // d_mem
    return pl.pallas_call(
        _mem_attn_kernel,
        out_shape=jax.ShapeDtypeStruct((b, t, d_mem), BF16),
        grid=(b, t // tq),
        in_specs=[
            pl.BlockSpec((None, tq, d_mem), lambda bi, i: (bi, i, qb)),
            pl.BlockSpec((None, tq, d_mem), lambda bi, i: (bi, i, qb + 1)),
            pl.BlockSpec((None, mt, n), lambda bi, i: (bi, 0, 0)),
        ],
        out_specs=pl.BlockSpec((None, tq, d_mem), lambda bi, i: (bi, i, 0)),
        compiler_params=_cparams(("parallel", "parallel")),
        name="mem_attn",
    )(proj3, proj3, kv)


def _out_proj_kernel(x_ref, gi_ref, bi_ref, hal_ref, ha_ref, yf_ref, yb_ref, z_ref, ng_ref, hm_ref, w_ref, g_ref, b_ref,
                     o_ref):
    dal = hal_ref.shape[1]
    da = dal + ha_ref.shape[1]
    dsm = yf_ref.shape[1]
    out = _dot(hal_ref[...], w_ref[0:dal, :])
    out = out + _dot(ha_ref[...], w_ref[dal:da, :])
    out = out + _dot(hm_ref[...], w_ref[da + dsm:, :])
    for g in range(SSM_GROUPS):
        c0 = g * GROUP_WIDTH
        yg = ((yf_ref[:, c0:c0 + GROUP_WIDTH].astype(F32) + yb_ref[:, c0:c0 + GROUP_WIDTH].astype(F32))
              * _silu(z_ref[:, c0:c0 + GROUP_WIDTH].astype(F32)))
        ms = jnp.mean(yg * yg, axis=-1, keepdims=True)
        hg = (yg * lax.rsqrt(ms + EPS) * ng_ref[:, c0:c0 + GROUP_WIDTH]).astype(BF16)
        out = out + _dot(hg, w_ref[da + c0:da + c0 + GROUP_WIDTH, :])
    xn = _layer_norm(x_ref[...], gi_ref[...], bi_ref[...])
    o_ref[...] = _layer_norm(ALPHA * xn + out, g_ref[...], b_ref[...])


def _out_proj(x2, ln_in_g, ln_in_b, h_att_lo, h_att, y_f, y_b, proj2, norm_g, h_mem, w_out, ln_g, ln_b, *, z_col0, tm):
    m, d = x2.shape
    dal, da, dsm, dm = h_att_lo.shape[1], h_att.shape[1], y_f.shape[1], h_mem.shape[1]
    zb = z_col0 // dsm
    row = lambda i: (i, 0)
    const = lambda i: (0, 0)
    return pl.pallas_call(
        _out_proj_kernel,
        out_shape=jax.ShapeDtypeStruct((m, d), F32),
        grid=(m // tm,),
        in_specs=[
            pl.BlockSpec((tm, d), row),
            pl.BlockSpec((1, d), const),
            pl.BlockSpec((1, d), const),
            pl.BlockSpec((tm, dal), row),
            pl.BlockSpec((tm, da), row),
            pl.BlockSpec((tm, dsm), row),
            pl.BlockSpec((tm, dsm), row),
            pl.BlockSpec((tm, dsm), lambda i: (i, zb)),
            pl.BlockSpec((1, dsm), const),
            pl.BlockSpec((tm, dm), row),
            pl.BlockSpec(w_out.shape, const, pipeline_mode=pl.Buffered(1)),
            pl.BlockSpec((1, d), const),
            pl.BlockSpec((1, d), const),
        ],
        out_specs=pl.BlockSpec((tm, d), row),
        compiler_params=_cparams(("parallel",)),
        name="out_proj_ln",
    )(x2, ln_in_g, ln_in_b, h_att_lo, h_att, y_f, y_b, proj2, norm_g, h_mem, w_out, ln_g, ln_b)


def _tile(n, pref):
    t = min(n, pref)
    while n % t:
        t //= 2
    return t


def _trunk(x, mem, p):
    b, t, d = x.shape
    m = b * t
    x2 = x.reshape(m, d)
    proj2, dt_raw = _in_proj(x2, p["ln_in_g"], p["ln_in_b"], p["w_main"], p["w_dt"],
                             tm=_tile(m, 1024), tn=512)
    n = proj2.shape[1]
    proj3 = proj2.reshape(b, t, n)
    d_att = ATT_HEADS * ATT_V_DIM
    d_ssm = SSM_HEADS * SSM_HEAD_DIM

    att = functools.partial(_attention, proj3, p["diff_lambda"], p["subln_g"], lam_init=p["lam_init"], kb=_tile(t, 512))
    h_att_lo = att(head0=0, nheads=ATT_STEEP_HEADS, qb=_tile(t, 512))
    h_att_hi = att(head0=ATT_STEEP_HEADS, nheads=ATT_HEADS - ATT_STEEP_HEADS, qb=_tile(t, 2048))

    xbc_act = _conv(proj3, p["conv_w"], p["conv_b"], col0=4 * d_att + d_ssm, tt=_tile(t, 512), tc=1024)
    y_f, y_b = _ssd(xbc_act, dt_raw.reshape(b, t, LANES), p["dtb_row"], p["alog_row"], p["dskip_row"],
                    p["r_f"], p["r_b"])

    kv = _mem_kv(mem, p["w_mem_kv"])
    h_mem = _mem_attn(proj3, kv, q_col0=4 * d_att + d_ssm + p["conv_w"].shape[1], tq=_tile(t, 512))

    y = _out_proj(x2, p["ln_in_g"], p["ln_in_b"], h_att_lo.reshape(m, -1), h_att_hi.reshape(m, -1), y_f.reshape(m, d_ssm),
                  y_b.reshape(m, d_ssm), proj2, p["ssm_norm_g"], h_mem.reshape(m, MEM_HEADS * MEM_HEAD_DIM),
                  p["w_out"], p["ln_g"], p["ln_b"], z_col0=4 * d_att, tm=_tile(m, 256))
    return y.reshape(b, t, d)


def _expansion_matrix(head0):
    rows = lax.broadcasted_iota(jnp.int32, (LANES, SSM_HEADS * SSM_HEAD_DIM), 0)
    cols = lax.broadcasted_iota(jnp.int32, (LANES, SSM_HEADS * SSM_HEAD_DIM), 1)
    return (rows == head0 + cols // SSM_HEAD_DIM).astype(BF16)


def kernel(x_prompt, x_sample, mem_prompt, mem_sample, ln_in_g, ln_in_b, w_in, conv_w, conv_b, dt_bias, a_log, d_skip, ssm_norm_g, diff_lambda, subln_g, w_mem_kv, w_out, ln_g, ln_b):
    assert w_in.shape[0] == DEPTH
    d_att = ATT_HEADS * ATT_V_DIM
    d_ssm = SSM_HEADS * SSM_HEAD_DIM
    d_xbc = conv_w.shape[-1]
    dt0 = 4 * d_att + d_ssm + d_xbc
    dt1 = dt0 + 2 * SSM_HEADS
    row = lambda v: v.reshape(1, -1).astype(F32)
    pad_heads = lambda v: jnp.pad(v.reshape(1, -1).astype(F32), ((0, 0), (0, LANES - 2 * SSM_HEADS)))

    l = 0
    w = w_in[l]
    p = {
        "ln_in_g": row(ln_in_g), "ln_in_b": row(ln_in_b),
        "w_main": jnp.concatenate([w[:, :dt0], w[:, dt1:]], axis=1).astype(BF16),
        "w_dt": jnp.pad(w[:, dt0:dt1], ((0, 0), (0, LANES - 2 * SSM_HEADS))).astype(BF16),
        "conv_w": conv_w[l].astype(F32), "conv_b": row(conv_b[l]),
        "dtb_row": pad_heads(dt_bias[l]), "alog_row": pad_heads(a_log[l]),
        "dskip_row": row(jnp.repeat(d_skip[l], SSM_HEAD_DIM)),
        "r_f": _expansion_matrix(0), "r_b": _expansion_matrix(SSM_HEADS),
        "ssm_norm_g": row(ssm_norm_g[l]),
        "diff_lambda": diff_lambda[l].astype(F32), "subln_g": row(subln_g[l]),
        "lam_init": 0.8 - 0.6 * math.exp(-0.3 * l),
        "w_mem_kv": w_mem_kv[l].astype(BF16), "w_out": w_out[l].astype(BF16),
        "ln_g": row(ln_g[l]), "ln_b": row(ln_b[l]),
    }
    return (_trunk(x_prompt, mem_prompt, p), _trunk(x_sample, mem_sample, p))
```

```python
import functools
import math

import jax
import jax.numpy as jnp
from jax import lax
from jax.experimental import pallas as pl
from jax.experimental.pallas import tpu as pltpu

F32 = jnp.float32
BF16 = jnp.bfloat16

DEPTH = 1
EPS = 1e-5
ATT_HEADS = 8
ATT_HEAD_DIM = 128
ATT_V_DIM = 2 * ATT_HEAD_DIM
ATT_STEEP_HEADS = 3
SSM_HEADS = 32
SSM_HEAD_DIM = 64
SSM_GROUPS = 4
SSM_STATE = 128
HEADS_PER_GROUP = SSM_HEADS // SSM_GROUPS
GROUP_WIDTH = HEADS_PER_GROUP * SSM_HEAD_DIM
D_CONV = 5
CHUNK = 128
MEM_HEADS = 4
MEM_HEAD_DIM = 128
LANES = 128
LOG2E = 1.4426950408889634
UNDERFLOW_LOG2 = 130.0
NORM_BOUND_SLACK = 1.01
FIXED_MAX_RANGE = 64.0
CONV_HALO_ROWS = 16
ALPHA = (2.0 * DEPTH) ** 0.25
VMEM_LIMIT_BYTES = 56 * 1024 * 1024


def _cparams(sem):
    return pltpu.CompilerParams(dimension_semantics=sem, vmem_limit_bytes=VMEM_LIMIT_BYTES)


def _layer_norm(x, g, b):
    mu = jnp.mean(x, axis=-1, keepdims=True)
    xc = x - mu
    var = jnp.mean(xc * xc, axis=-1, keepdims=True)
    return xc * lax.rsqrt(var + EPS) * g + b


def _silu(x):
    return x * jax.nn.sigmoid(x)


def _dot(a, b):
    return jnp.dot(a, b, preferred_element_type=F32)


def _dot_nt(a, b):
    return lax.dot_general(a, b, (((1,), (1,)), ((), ())), preferred_element_type=F32)


def _in_proj_kernel(x_ref, g_ref, b_ref, w_ref, wdt_ref, proj_ref, dt_ref, xn_ref):
    @pl.when(pl.program_id(1) == 0)
    def _():
        xn = _layer_norm(x_ref[...], g_ref[...], b_ref[...]).astype(BF16)
        xn_ref[...] = xn
        dt_ref[...] = _dot(xn, wdt_ref[...])

    proj_ref[...] = _dot(xn_ref[...], w_ref[...]).astype(BF16)


def _in_proj(x2, ln_g, ln_b, w_main, w_dt, *, tm, tn):
    m, d = x2.shape
    n = w_main.shape[1]
    return pl.pallas_call(
        _in_proj_kernel,
        out_shape=(jax.ShapeDtypeStruct((m, n), BF16), jax.ShapeDtypeStruct((m, LANES), F32)),
        grid=(m // tm, n // tn),
        in_specs=[
            pl.BlockSpec((tm, d), lambda i, j: (i, 0)),
            pl.BlockSpec((1, d), lambda i, j: (0, 0)),
            pl.BlockSpec((1, d), lambda i, j: (0, 0)),
            pl.BlockSpec((d, tn), lambda i, j: (0, j)),
            pl.BlockSpec((d, LANES), lambda i, j: (0, 0)),
        ],
        out_specs=(
            pl.BlockSpec((tm, tn), lambda i, j: (i, j)),
            pl.BlockSpec((tm, LANES), lambda i, j: (i, 0)),
        ),
        scratch_shapes=[pltpu.VMEM((tm, d), BF16)],
        compiler_params=_cparams(("parallel", "arbitrary")),
        name="in_proj",
    )(x2, ln_g, ln_b, w_main, w_dt)


def _attn_kernel(dl_ref, subg_ref, q_ref, k_ref, v_ref, g_ref, o_ref, acc_ref, qt_ref, vt_ref, st_ref, kn_ref, l_ref,
                 *, qb, kb, head0, lam_init):
    h = head0 + pl.program_id(1)
    i = pl.program_id(2)
    nblk = k_ref.shape[0] // kb
    ratio = qb // kb
    d = ATT_HEAD_DIM
    c = jnp.exp2(-(h + 1).astype(F32)) * LOG2E

    @pl.when(i == 0)
    def _():
        def tr(jj, kn2):
            start = pl.multiple_of(jj * kb, kb)
            vt_ref[jj] = v_ref[pl.ds(start, kb), :].astype(F32).T.astype(BF16)
            kf = k_ref[pl.ds(start, kb), :].astype(F32)
            sq = kf * kf
            for mp in range(2):
                rows = jnp.sum(sq[:, mp * d:(mp + 1) * d], axis=1, keepdims=True)
                kn2 = jnp.maximum(kn2, jnp.max(rows, axis=0, keepdims=True))
            return kn2
        kn2 = lax.fori_loop(0, nblk, tr, jnp.zeros((1, 1), F32))
        kn_ref[...] = jnp.broadcast_to(kn2, kn_ref.shape)

    def split3(x):
        hi = x.astype(BF16).astype(F32)
        mid = (x - hi).astype(BF16).astype(F32)
        return hi, mid, x - hi - mid

    q = q_ref[...].astype(F32) * (LOG2E / math.sqrt(d))
    qt, qrow2 = [], []
    for mp in range(2):
        qtf = q[:, mp * d:(mp + 1) * d].T
        qrow2.append(jnp.sum(qtf * qtf, axis=0, keepdims=True))
        qt.append(qtf.astype(BF16))
    qn2 = jnp.max(jnp.maximum(qrow2[0], qrow2[1]), axis=1, keepdims=True)
    kmax = jnp.sqrt(kn_ref[0:1, 0:1]) * NORM_BOUND_SLACK

    bound = jnp.sqrt(qn2) * kmax
    fixed = 2.0 * bound <= FIXED_MAX_RANGE
    reach = (jnp.where(fixed, 0.0, 2.0 * bound) + UNDERFLOW_LOG2) / c
    q0 = (i * qb).astype(F32)
    seq_last = float(k_ref.shape[0] - 1)
    first = jnp.floor(jnp.maximum(q0 - reach, 0.0) * (1.0 / kb)).astype(jnp.int32)[0, 0]
    last = jnp.floor(jnp.minimum(q0 + (qb - 1) + reach, seq_last) * (1.0 / kb)).astype(jnp.int32)[0, 0]
    j_lo = jnp.minimum(first, i * ratio)
    j_hi = jnp.maximum(last + 1, (i + 1) * ratio)
    fixed_ok = fixed.astype(jnp.int32)[0, 0]

    acc_ref[...] = jnp.zeros_like(acc_ref)
    qw = st_ref.shape[1]
    nq = qb // qw
    chains = [(mp, hf) for mp in range(2) for hf in range(nq)]
    row = lax.broadcasted_iota(jnp.int32, (d, qb), 0)
    lane = lax.broadcasted_iota(jnp.int32, (kb, d), 1)
    cr = c * lax.broadcasted_iota(jnp.int32, (d, qb), 1).astype(F32)
    upos = lax.broadcasted_iota(jnp.int32, (kb, d), 0).astype(F32)

    def query_aug(x, ones):
        x = split3(x)
        return jnp.where(row == 0, x[0], jnp.where(row == 1, x[1], jnp.where(row == 2, x[2],
                         jnp.where(row < 6, ones, 0.0)))).astype(BF16)

    def key_aug(y):
        y = split3(y)
        return jnp.where(lane < 3, -1.0, jnp.where(lane == 3, y[0], jnp.where(lane == 4, y[1],
                         jnp.where(lane == 5, y[2], 0.0)))).astype(BF16)

    def key_operand(j, mp, aug):
        start = pl.multiple_of(j * kb, kb)
        return jnp.concatenate([k_ref[pl.ds(start, kb), mp * d:(mp + 1) * d], aug], axis=1)

    def overlap_term(overlap, hf):
        ahead = (lax.broadcasted_iota(jnp.int32, (kb, qw), 0) - lax.broadcasted_iota(jnp.int32, (kb, qw), 1)
                 + (overlap - hf * qw))
        return (-2.0 * c) * jnp.maximum(ahead, 0).astype(F32)

    def run_chains(j, variant, next_variant, aug_of, finish):
        kp = [key_operand(j, mp, aug_of(j)) for mp in range(2)]
        results = {}
        pending = st_ref[...]
        for n, (mp, hf) in enumerate(chains):
            st = pending
            if n + 1 < len(chains):
                mp1, hf1 = chains[n + 1]
                pending = _dot(kp[mp1], qt_ref[2 * variant + mp1, :, hf1 * qw:(hf1 + 1) * qw])
            else:
                jn = jnp.minimum(j + 1, nblk - 1)
                st_ref[...] = _dot(key_operand(jn, 0, aug_of(jn)), qt_ref[2 * next_variant, :, 0:qw])
            results[(mp, hf)] = finish(mp, hf, st)
        return results

    def gather(results, idx):
        return [jnp.concatenate([results[(mp, hf)][idx] for hf in range(nq)], axis=1) for mp in range(2)]

    def fixed_path():
        for mp in range(2):
            m_ub = jnp.sqrt(qrow2[mp]) * kmax
            qt_ref[mp] = jnp.concatenate([qt[mp], query_aug(m_ub + cr, 1.0)], axis=0)
            qt_ref[2 + mp] = jnp.concatenate([qt[mp], query_aug(m_ub - cr, -1.0)], axis=0)

        def aug_of(j):
            return key_aug(c * ((j * kb - i * qb).astype(F32) + upos))

        def step(j, carry, variant, overlap, next_variant):
            vt = vt_ref[j]

            def finish(mp, hf, st):
                if overlap is not None:
                    st = st + overlap_term(overlap, hf)
                pt = jnp.exp2(st)
                acc_ref[mp, :, hf * qw:(hf + 1) * qw] += _dot(vt, pt.astype(BF16))
                return (jnp.sum(pt, axis=0, keepdims=True),)

            sums = gather(run_chains(j, variant, next_variant, aug_of, finish), 0)
            return (carry[0] + sums[0], carry[1] + sums[1])

        zero = jnp.zeros((1, qb), F32)
        st_ref[...] = _dot(key_operand(j_lo, 0, aug_of(j_lo)), qt_ref[0, :, 0:qw])
        carry = lax.fori_loop(j_lo, i * ratio, lambda j, cy: step(j, cy, 0, None, 0), (zero, zero))
        for jj in range(ratio):
            carry = step(i * ratio + jj, carry, 0, jj * kb, 0 if jj + 1 < ratio else 1)
        carry = lax.fori_loop((i + 1) * ratio, j_hi, lambda j, cy: step(j, cy, 1, None, 1), carry)
        l_ref[0] = carry[0]
        l_ref[1] = carry[1]

    def running_path():
        aug_k = key_aug(c * upos)
        for mp in range(2):
            qt_ref[mp] = jnp.concatenate([qt[mp], query_aug(cr, 1.0)], axis=0)
            qt_ref[2 + mp] = jnp.concatenate([qt[mp], query_aug(-cr, -1.0)], axis=0)

        def step(j, carry, variant, delta, overlap, next_variant):
            vt = vt_ref[j]

            def finish(mp, hf, st):
                if overlap is not None:
                    st = st + overlap_term(overlap, hf)
                m_old, l_old = carry[2 * mp][:, hf * qw:(hf + 1) * qw], carry[2 * mp + 1][:, hf * qw:(hf + 1) * qw]
                m_new = jnp.maximum(m_old, jnp.max(st, axis=0, keepdims=True) + delta)
                a = jnp.exp2(m_old - m_new)
                pt = jnp.exp2(st - (m_new - delta))
                l_new = a * l_old + jnp.sum(pt, axis=0, keepdims=True)
                acc_ref[mp, :, hf * qw:(hf + 1) * qw] = (a * acc_ref[mp, :, hf * qw:(hf + 1) * qw]
                                                         + _dot(vt, pt.astype(BF16)))
                return m_new, l_new

            res = run_chains(j, variant, next_variant, lambda jx: aug_k, finish)
            m_new, l_new = gather(res, 0), gather(res, 1)
            return (m_new[0], l_new[0], m_new[1], l_new[1])

        def below(j, carry):
            return step(j, carry, 0, -c * (i * qb - j * kb).astype(F32), None, 0)

        def above(j, carry):
            return step(j, carry, 1, -c * (j * kb - i * qb).astype(F32), None, 1)

        neg = jnp.full((1, qb), -1e30, F32)
        zero = jnp.zeros((1, qb), F32)
        st_ref[...] = _dot(key_operand(j_lo, 0, aug_k), qt_ref[0, :, 0:qw])
        carry = lax.fori_loop(j_lo, i * ratio, below, (neg, zero, neg, zero))
        for jj in range(ratio):
            carry = step(i * ratio + jj, carry, 0, c * float(jj * kb), jj * kb, 0 if jj + 1 < ratio else 1)
        carry = lax.fori_loop((i + 1) * ratio, j_hi, above, carry)
        l_ref[0] = carry[1]
        l_ref[1] = carry[3]

    pl.when(fixed_ok == 1)(fixed_path)
    pl.when(fixed_ok == 0)(running_path)
    l0, l1 = l_ref[0], l_ref[1]

    dl = dl_ref[...]
    lam = (jnp.exp(jnp.sum(dl[0:1] * dl[1:2], axis=-1, keepdims=True))
           - jnp.exp(jnp.sum(dl[2:3] * dl[3:4], axis=-1, keepdims=True)) + lam_init)
    out = (acc_ref[0] / l0 - lam * (acc_ref[1] / l1)).T
    ms = jnp.mean(out * out, axis=-1, keepdims=True)
    out = out * lax.rsqrt(ms + EPS) * subg_ref[...] * (1.0 - lam_init)
    o_ref[...] = (out * _silu(g_ref[...].astype(F32))).astype(BF16)


def _attention(proj3, diff_lambda, subln_g, *, lam_init, head0, nheads, qb, kb):
    b, t, _ = proj3.shape
    hh = ATT_HEADS
    tq = qb
    kern = functools.partial(_attn_kernel, qb=qb, kb=kb, head0=head0, lam_init=lam_init)
    return pl.pallas_call(
        kern,
        out_shape=jax.ShapeDtypeStruct((b, t, nheads * ATT_V_DIM), BF16),
        grid=(b, nheads, t // tq),
        in_specs=[
            pl.BlockSpec((4, ATT_HEAD_DIM), lambda bi, h, i: (0, 0)),
            pl.BlockSpec((1, ATT_V_DIM), lambda bi, h, i: (0, 0)),
            pl.BlockSpec((None, tq, ATT_V_DIM), lambda bi, h, i: (bi, i, head0 + h)),
            pl.BlockSpec((None, t, ATT_V_DIM), lambda bi, h, i: (bi, 0, hh + head0 + h)),
            pl.BlockSpec((None, t, ATT_V_DIM), lambda bi, h, i: (bi, 0, 2 * hh + head0 + h)),
            pl.BlockSpec((None, tq, ATT_V_DIM), lambda bi, h, i: (bi, i, 3 * hh + head0 + h)),
        ],
        out_specs=pl.BlockSpec((None, tq, ATT_V_DIM), lambda bi, h, i: (bi, i, h)),
        scratch_shapes=[
            pltpu.VMEM((2, ATT_V_DIM, qb), F32),
            pltpu.VMEM((4, 2 * ATT_HEAD_DIM, qb), BF16),
            pltpu.VMEM((t // kb, ATT_V_DIM, kb), BF16),
            pltpu.VMEM((kb, min(qb, 2 * LANES)), F32),
            pltpu.VMEM((8, LANES), F32),
            pltpu.VMEM((2, 1, qb), F32),
        ],
        compiler_params=_cparams(("parallel", "parallel", "arbitrary")),
        name="diff_attn",
    )(diff_lambda, subln_g, proj3, proj3, proj3, proj3)


def _conv_kernel(cur_ref, prev_ref, next_ref, w_ref, b_ref, o_ref, u_ref, *, tt):
    i = pl.program_id(1)
    halo = CONV_HALO_ROWS
    pad = (D_CONV - 1) // 2
    prev = prev_ref[...].astype(F32)
    nxt = next_ref[...].astype(F32)
    u_ref[0:halo, :] = jnp.where(i > 0, prev, 0.0)
    u_ref[halo:halo + tt, :] = cur_ref[...].astype(F32)
    u_ref[halo + tt:2 * halo + tt, :] = jnp.where(i < pl.num_programs(1) - 1, nxt, 0.0)
    u = u_ref[...]
    n = tt + 2 * halo
    down = lambda x: pltpu.roll(x, 1, 0)
    up = lambda x: pltpu.roll(x, n - 1, 0)
    before = u * w_ref[0:1, :]
    for k in range(1, pad):
        before = down(before) + u * w_ref[k:k + 1, :]
    after = u * w_ref[D_CONV - 1:D_CONV, :]
    for k in range(D_CONV - 2, pad, -1):
        after = up(after) + u * w_ref[k:k + 1, :]
    acc = u * w_ref[pad:pad + 1, :] + down(before) + up(after)
    o_ref[...] = _silu(acc[halo:halo + tt] + b_ref[...]).astype(BF16)


def _conv(proj3, conv_w, conv_b, *, col0, tt, tc):
    b, t, _ = proj3.shape
    width = conv_w.shape[1]
    halo = CONV_HALO_ROWS
    cb0 = col0 // tc
    rpb = tt // halo
    last = t // halo - 1
    kern = functools.partial(_conv_kernel, tt=tt)
    return pl.pallas_call(
        kern,
        out_shape=jax.ShapeDtypeStruct((b, t, width), BF16),
        grid=(b, t // tt, width // tc),
        in_specs=[
            pl.BlockSpec((None, tt, tc), lambda bi, i, c: (bi, i, cb0 + c)),
            pl.BlockSpec((None, halo, tc), lambda bi, i, c: (bi, jnp.maximum(i * rpb - 1, 0), cb0 + c)),
            pl.BlockSpec((None, halo, tc), lambda bi, i, c: (bi, jnp.minimum((i + 1) * rpb, last), cb0 + c)),
            pl.BlockSpec((D_CONV, tc), lambda bi, i, c: (0, c)),
            pl.BlockSpec((1, tc), lambda bi, i, c: (0, c)),
        ],
        out_specs=pl.BlockSpec((None, tt, tc), lambda bi, i, c: (bi, i, c)),
        scratch_shapes=[pltpu.VMEM((tt + 2 * halo, tc), F32)],
        compiler_params=_cparams(("parallel", "parallel", "parallel")),
        name="conv_silu",
    )(proj3, proj3, proj3, conv_w, conv_b)


def _cumsum_rows(x):
    row = lax.broadcasted_iota(jnp.int32, x.shape, 0)
    sh = 1
    while sh < x.shape[0]:
        x = x + jnp.where(row >= sh, pltpu.roll(x, sh, 0), 0.0)
        sh *= 2
    return x


def _ssd_direction(xa_ref, dtr_ref, dtb_ref, alog_ref, r_ref, s_ref, y_ref, dskip_ref, *, head0, backward):
    L = CHUNK
    d_ssm = SSM_HEADS * SSM_HEAD_DIM
    n_bc = SSM_GROUPS * SSM_STATE
    dt = jax.nn.softplus(dtr_ref[...] + dtb_ref[...])
    a = dt * (-jnp.exp(alog_ref[...]))
    p = _cumsum_rows(a)
    tot = p[L - 1:L, :]
    ac = (tot - p + a) if backward else p
    eo = jnp.exp(ac)
    ds = jnp.exp(tot - ac)
    cd = jnp.broadcast_to(jnp.exp(tot), (8, LANES))
    stack = jnp.concatenate([dt, eo, ds, cd], axis=0)
    ex = _dot(stack.astype(BF16), r_ref[...])
    dtx, eox, dsx, cdx = ex[0:L], ex[L:2 * L], ex[2 * L:3 * L], ex[3 * L:3 * L + 1]

    xs = xa_ref[:, 0:d_ssm].astype(F32)
    xd = xs * dtx
    xdb = xd.astype(BF16)
    wst = (xd * dsx).astype(BF16)
    ac_t = ac.T
    row = lax.broadcasted_iota(jnp.int32, (L, L), 0)
    col = lax.broadcasted_iota(jnp.int32, (L, L), 1)
    mask = (row <= col) if backward else (row >= col)
    lane_lo = col < SSM_HEAD_DIM
    zero_b = jnp.zeros((L, L), BF16)

    for g in range(SSM_GROUPS):
        c0 = g * GROUP_WIDTH
        bg = xa_ref[:, d_ssm + g * SSM_STATE:d_ssm + (g + 1) * SSM_STATE]
        cg = xa_ref[:, d_ssm + n_bc + g * SSM_STATE:d_ssm + n_bc + (g + 1) * SSM_STATE]
        cb = _dot_nt(cg, bg)
        st = s_ref[g]
        y = _dot(cg, st.astype(BF16)) * eox[:, c0:c0 + GROUP_WIDTH]
        if dskip_ref is not None:
            y = y + xs[:, c0:c0 + GROUP_WIDTH] * dskip_ref[:, c0:c0 + GROUP_WIDTH]
        pieces = []
        for pp in range(HEADS_PER_GROUP // 2):
            ms = []
            for e in (2 * pp, 2 * pp + 1):
                hc = head0 + g * HEADS_PER_GROUP + e
                seg = ac[:, hc:hc + 1] - ac_t[hc:hc + 1, :]
                lm = jnp.exp(jnp.where(mask, seg, -jnp.inf))
                ms.append((cb * lm).astype(BF16))
            xpair = xdb[:, c0 + pp * LANES:c0 + (pp + 1) * LANES]
            rhs = jnp.concatenate([jnp.where(lane_lo, xpair, zero_b), jnp.where(lane_lo, zero_b, xpair)], axis=0)
            pieces.append(_dot(jnp.concatenate(ms, axis=1), rhs))
        y = y + jnp.concatenate(pieces, axis=1)
        y_ref[:, c0:c0 + GROUP_WIDTH] = y.astype(BF16)
        bt = bg.astype(F32).T.astype(BF16)
        s_ref[g] = st * cdx[:, c0:c0 + GROUP_WIDTH] + _dot(bt, wst[:, c0:c0 + GROUP_WIDTH])


def _ssd_kernel(dtb_ref, alog_ref, dskip_ref, rf_ref, rb_ref, xaf_ref, dtf_ref, xab_ref, dtbk_ref,
                yf_ref, yb_ref, sf_ref, sb_ref):
    @pl.when(pl.program_id(1) == 0)
    def _():
        sf_ref[...] = jnp.zeros_like(sf_ref)
        sb_ref[...] = jnp.zeros_like(sb_ref)

    _ssd_direction(xaf_ref, dtf_ref, dtb_ref, alog_ref, rf_ref, sf_ref, yf_ref, dskip_ref,
                   head0=0, backward=False)
    _ssd_direction(xab_ref, dtbk_ref, dtb_ref, alog_ref, rb_ref, sb_ref, yb_ref, None,
                   head0=SSM_HEADS, backward=True)


def _ssd(xbc_act, dt_raw3, dtb_row, alog_row, dskip_row, r_f, r_b):
    b, t, wa = xbc_act.shape
    nc = t // CHUNK
    d_ssm = SSM_HEADS * SSM_HEAD_DIM
    const = lambda bi, c: (0, 0)
    return pl.pallas_call(
        _ssd_kernel,
        out_shape=(jax.ShapeDtypeStruct((b, t, d_ssm), BF16), jax.ShapeDtypeStruct((b, t, d_ssm), BF16)),
        grid=(b, nc),
        in_specs=[
            pl.BlockSpec((1, LANES), const),
            pl.BlockSpec((1, LANES), const),
            pl.BlockSpec((1, d_ssm), const),
            pl.BlockSpec((LANES, d_ssm), const),
            pl.BlockSpec((LANES, d_ssm), const),
            pl.BlockSpec((None, CHUNK, wa), lambda bi, c: (bi, c, 0)),
            pl.BlockSpec((None, CHUNK, LANES), lambda bi, c: (bi, c, 0)),
            pl.BlockSpec((None, CHUNK, wa), lambda bi, c: (bi, nc - 1 - c, 0)),
            pl.BlockSpec((None, CHUNK, LANES), lambda bi, c: (bi, nc - 1 - c, 0)),
        ],
        out_specs=(
            pl.BlockSpec((None, CHUNK, d_ssm), lambda bi, c: (bi, c, 0)),
            pl.BlockSpec((None, CHUNK, d_ssm), lambda bi, c: (bi, nc - 1 - c, 0)),
        ),
        scratch_shapes=[
            pltpu.VMEM((SSM_GROUPS, SSM_STATE, GROUP_WIDTH), F32),
            pltpu.VMEM((SSM_GROUPS, SSM_STATE, GROUP_WIDTH), F32),
        ],
        compiler_params=_cparams(("parallel", "arbitrary")),
        name="ssd_scan",
    )(dtb_row, alog_row, dskip_row, r_f, r_b, xbc_act, dt_raw3, xbc_act, dt_raw3)


def _mem_kv_kernel(mem_ref, w_ref, o_ref):
    o_ref[...] = _dot(mem_ref[...].astype(BF16), w_ref[...]).astype(BF16)


def _mem_kv(mem, w_kv):
    b, mt, d = mem.shape
    n = w_kv.shape[1]
    return pl.pallas_call(
        _mem_kv_kernel,
        out_shape=jax.ShapeDtypeStruct((b, mt, n), BF16),
        grid=(b,),
        in_specs=[
            pl.BlockSpec((None, mt, d), lambda bi: (bi, 0, 0)),
            pl.BlockSpec((d, n), lambda bi: (0, 0)),
        ],
        out_specs=pl.BlockSpec((None, mt, n), lambda bi: (bi, 0, 0)),
        compiler_params=_cparams(("parallel",)),
        name="mem_kv_proj",
    )(mem, w_kv)


def _mem_attn_kernel(q_ref, g_ref, kv_ref, o_ref):
    d = MEM_HEAD_DIM
    d_mem = MEM_HEADS * d
    scale = 1.0 / math.sqrt(d)
    for hh in range(MEM_HEADS):
        q = q_ref[:, hh * d:(hh + 1) * d]
        k = kv_ref[:, hh * d:(hh + 1) * d]
        v = kv_ref[:, d_mem + hh * d:d_mem + (hh + 1) * d]
        s = _dot_nt(q, k) * scale
        p = jnp.exp(s - jnp.max(s, axis=-1, keepdims=True))
        o = _dot(p.astype(BF16), v) / jnp.sum(p, axis=-1, keepdims=True)
        o_ref[:, hh * d:(hh + 1) * d] = (o * _silu(g_ref[:, hh * d:(hh + 1) * d].astype(F32))).astype(BF16)


def _mem_attn(proj3, kv, *, q_col0, tq):
    b, t, _ = proj3.shape
    mt, n = kv.shape[1:]
    d_mem = MEM_HEADS * MEM_HEAD_DIM
    qb = q_col0 // d_mem
    return pl.pallas_call(
        _mem_attn_kernel,
        out_shape=jax.ShapeDtypeStruct((b, t, d_mem), BF16),
        grid=(b, t // tq),
        in_specs=[
            pl.BlockSpec((None, tq, d_mem), lambda bi, i: (bi, i, qb)),
            pl.BlockSpec((None, tq, d_mem), lambda bi, i: (bi, i, qb + 1)),
            pl.BlockSpec((None, mt, n), lambda bi, i: (bi, 0, 0)),
        ],
        out_specs=pl.BlockSpec((None, tq, d_mem), lambda bi, i: (bi, i, 0)),
        compiler_params=_cparams(("parallel", "parallel")),
        name="mem_attn",
    )(proj3, proj3, kv)


def _out_proj_kernel(x_ref, gi_ref, bi_ref, hal_ref, ha_ref, yf_ref, yb_ref, z_ref, ng_ref, hm_ref, w_ref, g_ref, b_ref,
                     o_ref):
    dal = hal_ref.shape[1]
    da = dal + ha_ref.shape[1]
    dsm = yf_ref.shape[1]
    out = _dot(hal_ref[...], w_ref[0:dal, :])
    out = out + _dot(ha_ref[...], w_ref[dal:da, :])
    out = out + _dot(hm_ref[...], w_ref[da + dsm:, :])
    for g in range(SSM_GROUPS):
        c0 = g * GROUP_WIDTH
        yg = ((yf_ref[:, c0:c0 + GROUP_WIDTH].astype(F32) + yb_ref[:, c0:c0 + GROUP_WIDTH].astype(F32))
              * _silu(z_ref[:, c0:c0 + GROUP_WIDTH].astype(F32)))
        ms = jnp.mean(yg * yg, axis=-1, keepdims=True)
        hg = (yg * lax.rsqrt(ms + EPS) * ng_ref[:, c0:c0 + GROUP_WIDTH]).astype(BF16)
        out = out + _dot(hg, w_ref[da + c0:da + c0 + GROUP_WIDTH, :])
    xn = _layer_norm(x_ref[...], gi_ref[...], bi_ref[...])
    o_ref[...] = _layer_norm(ALPHA * xn + out, g_ref[...], b_ref[...])


def _out_proj(x2, ln_in_g, ln_in_b, h_att_lo, h_att, y_f, y_b, proj2, norm_g, h_mem, w_out, ln_g, ln_b, *, z_col0, tm):
    m, d = x2.shape
    dal, da, dsm, dm = h_att_lo.shape[1], h_att.shape[1], y_f.shape[1], h_mem.shape[1]
    zb = z_col0 // dsm
    row = lambda i: (i, 0)
    const = lambda i: (0, 0)
    return pl.pallas_call(
        _out_proj_kernel,
        out_shape=jax.ShapeDtypeStruct((m, d), F32),
        grid=(m // tm,),
        in_specs=[
            pl.BlockSpec((tm, d), row),
            pl.BlockSpec((1, d), const),
            pl.BlockSpec((1, d), const),
            pl.BlockSpec((tm, dal), row),
            pl.BlockSpec((tm, da), row),
            pl.BlockSpec((tm, dsm), row),
            pl.BlockSpec((tm, dsm), row),
            pl.BlockSpec((tm, dsm), lambda i: (i, zb)),
            pl.BlockSpec((1, dsm), const),
            pl.BlockSpec((tm, dm), row),
            pl.BlockSpec(w_out.shape, const, pipeline_mode=pl.Buffered(1)),
            pl.BlockSpec((1, d), const),
            pl.BlockSpec((1, d), const),
        ],
        out_specs=pl.BlockSpec((tm, d), row),
        compiler_params=_cparams(("parallel",)),
        name="out_proj_ln",
    )(x2, ln_in_g, ln_in_b, h_att_lo, h_att, y_f, y_b, proj2, norm_g, h_mem, w_out, ln_g, ln_b)


def _tile(n, pref):
    t = min(n, pref)
    while n % t:
        t //= 2
    return t


def _trunk(x, mem, p):
    b, t, d = x.shape
    m = b * t
    x2 = x.reshape(m, d)
    proj2, dt_raw = _in_proj(x2, p["ln_in_g"], p["ln_in_b"], p["w_main"], p["w_dt"],
                             tm=_tile(m, 1024), tn=512)
    n = proj2.shape[1]
    proj3 = proj2.reshape(b, t, n)
    d_att = ATT_HEADS * ATT_V_DIM
    d_ssm = SSM_HEADS * SSM_HEAD_DIM

    att = functools.partial(_attention, proj3, p["diff_lambda"], p["subln_g"], lam_init=p["lam_init"], kb=_tile(t, 512))
    h_att_lo = att(head0=0, nheads=ATT_STEEP_HEADS, qb=_tile(t, 512 if t > 2048 else 2048))
    h_att_hi = att(head0=ATT_STEEP_HEADS, nheads=ATT_HEADS - ATT_STEEP_HEADS, qb=_tile(t, 2048))

    xbc_act = _conv(proj3, p["conv_w"], p["conv_b"], col0=4 * d_att + d_ssm, tt=_tile(t, 512), tc=1024)
    y_f, y_b = _ssd(xbc_act, dt_raw.reshape(b, t, LANES), p["dtb_row"], p["alog_row"], p["dskip_row"],
                    p["r_f"], p["r_b"])

    kv = _mem_kv(mem, p["w_mem_kv"])
    h_mem = _mem_attn(proj3, kv, q_col0=4 * d_att + d_ssm + p["conv_w"].shape[1], tq=_tile(t, 512))

    y = _out_proj(x2, p["ln_in_g"], p["ln_in_b"], h_att_lo.reshape(m, -1), h_att_hi.reshape(m, -1), y_f.reshape(m, d_ssm),
                  y_b.reshape(m, d_ssm), proj2, p["ssm_norm_g"], h_mem.reshape(m, MEM_HEADS * MEM_HEAD_DIM),
                  p["w_out"], p["ln_g"], p["ln_b"], z_col0=4 * d_att, tm=_tile(m, 256))
    return y.reshape(b, t, d)


def _expansion_matrix(head0):
    rows = lax.broadcasted_iota(jnp.int32, (LANES, SSM_HEADS * SSM_HEAD_DIM), 0)
    cols = lax.broadcasted_iota(jnp.int32, (LANES, SSM_HEADS * SSM_HEAD_DIM), 1)
    return (rows == head0 + cols // SSM_HEAD_DIM).astype(BF16)


def kernel(x_prompt, x_sample, mem_prompt, mem_sample, ln_in_g, ln_in_b, w_in, conv_w, conv_b, dt_bias, a_log, d_skip, ssm_norm_g, diff_lambda, subln_g, w_mem_kv, w_out, ln_g, ln_b):
    assert w_in.shape[0] == DEPTH
    d_att = ATT_HEADS * ATT_V_DIM
    d_ssm = SSM_HEADS * SSM_HEAD_DIM
    d_xbc = conv_w.shape[-1]
    dt0 = 4 * d_att + d_ssm + d_xbc
    dt1 = dt0 + 2 * SSM_HEADS
    row = lambda v: v.reshape(1, -1).astype(F32)
    pad_heads = lambda v: jnp.pad(v.reshape(1, -1).astype(F32), ((0, 0), (0, LANES - 2 * SSM_HEADS)))

    l = 0
    w = w_in[l]
    p = {
        "ln_in_g": row(ln_in_g), "ln_in_b": row(ln_in_b),
        "w_main": jnp.concatenate([w[:, :dt0], w[:, dt1:]], axis=1).astype(BF16),
        "w_dt": jnp.pad(w[:, dt0:dt1], ((0, 0), (0, LANES - 2 * SSM_HEADS))).astype(BF16),
        "conv_w": conv_w[l].astype(F32), "conv_b": row(conv_b[l]),
        "dtb_row": pad_heads(dt_bias[l]), "alog_row": pad_heads(a_log[l]),
        "dskip_row": row(jnp.repeat(d_skip[l], SSM_HEAD_DIM)),
        "r_f": _expansion_matrix(0), "r_b": _expansion_matrix(SSM_HEADS),
        "ssm_norm_g": row(ssm_norm_g[l]),
        "diff_lambda": diff_lambda[l].astype(F32), "subln_g": row(subln_g[l]),
        "lam_init": 0.8 - 0.6 * math.exp(-0.3 * l),
        "w_mem_kv": w_mem_kv[l].astype(BF16), "w_out": w_out[l].astype(BF16),
        "ln_g": row(ln_g[l]), "ln_b": row(ln_b[l]),
    }
    return (_trunk(x_prompt, mem_prompt, p), _trunk(x_sample, mem_sample, p))
```
